```python
import math
import jax, jax.numpy as jnp
from jax import lax
import numpy as np

D_MODEL = 2048
BATCH = 4
SEQ = 4096
DEPTH = 2

N_MIXERS = 2
SB_HEADS = 16
SB_HEAD_DIM = D_MODEL // SB_HEADS
DA_HEADS = 8
DA_QK_DIM = D_MODEL // (2 * DA_HEADS)
DA_V_DIM = 2 * DA_QK_DIM
QBLK = 128
N_EXPERTS = 64
TOP_K = 6
N_GROUPS = 8
TOPK_GROUPS = 4
EXPERT_DIM = D_MODEL // 4
SHARED_DIM = EXPERT_DIM
ROUTED_SCALE = 2.5
MOE_BLK = 128
EPS = 1e-6
SUBLN_EPS = 1e-5

kernel_name = 'stickbreak_diffattn_moe_hybrid'


def rms_norm(x, g, eps=EPS):
    xf = x.astype(jnp.float32)
    y = xf * lax.rsqrt(jnp.mean(xf * xf, axis=-1, keepdims=True) + eps)
    return (y * g.astype(jnp.float32)).astype(x.dtype)


def _query_blocks(q):
    b, h, s = q.shape[:3]
    nb = s // QBLK
    qb = q.reshape((b, h, nb, QBLK) + q.shape[3:])
    return jnp.moveaxis(qb, 2, 0), nb


def _merge_blocks(out, b, h, s):
    return jnp.moveaxis(out, 0, 2).reshape(b, h, s, out.shape[-1])


def stick_breaking_attention(q, k, v):
    b, h, s, _ = q.shape
    scale = SB_HEAD_DIM ** -0.5
    qb, nb = _query_blocks(q)
    kpos = jnp.arange(s)

    def block(args):
        qi, bi = args
        z = jnp.einsum('bhqd,bhkd->bhqk', qi, k).astype(jnp.float32) * scale
        qpos = bi * QBLK + jnp.arange(QBLK)
        earlier = kpos[None, :] < qpos[:, None]
        log_keep = jnp.where(earlier, -jax.nn.softplus(z), 0.0)
        between = lax.cumsum(log_keep, axis=3, reverse=True) - log_keep
        a = jnp.where(earlier, jnp.exp(jax.nn.log_sigmoid(z) + between), 0.0)
        return jnp.einsum('bhqk,bhkd->bhqd', a.astype(v.dtype), v)

    out = lax.map(block, (qb, jnp.arange(nb)))
    return _merge_blocks(out, b, h, s)


def differential_attention(q, k, v, lam):
    b, h, s = q.shape[:3]
    scale = DA_QK_DIM ** -0.5
    slopes = jnp.asarray(2.0 ** (-8.0 * np.arange(1, DA_HEADS + 1) / DA_HEADS), jnp.float32)
    qb, nb = _query_blocks(q)
    kpos = jnp.arange(s)

    def block(args):
        qi, bi = args
        z = jnp.einsum('bhqcd,bhkcd->bchqk', qi, k).astype(jnp.float32) * scale
        qpos = bi * QBLK + jnp.arange(QBLK)
        dist = (qpos[:, None] - kpos[None, :]).astype(jnp.float32)
        z = jnp.where(dist >= 0, z - slopes[:, None, None] * dist, -jnp.inf)
        p = jax.nn.softmax(z, axis=-1)
        a = p[:, 0] - lam * p[:, 1]
        return jnp.einsum('bhqk,bhkd->bhqd', a.astype(v.dtype), v)

    out = lax.map(block, (qb, jnp.arange(nb)))
    return _merge_blocks(out, b, h, s)


def moe_ffn(h, w_router, router_bias, w_gate, w_up, w_down, ws_gate, ws_up, ws_down):
    t, d = h.shape
    scores = jax.nn.sigmoid((h @ w_router.T).astype(jnp.float32))
    biased = scores + router_bias.astype(jnp.float32)
    per_group = N_EXPERTS // N_GROUPS
    group_score = lax.top_k(biased.reshape(t, N_GROUPS, per_group), 2)[0].sum(-1)
    _, top_groups = lax.top_k(group_score, TOPK_GROUPS)
    group_keep = jax.nn.one_hot(top_groups, N_GROUPS).sum(1) > 0
    expert_keep = jnp.repeat(group_keep, per_group, axis=1)
    _, top_e = lax.top_k(jnp.where(expert_keep, biased, -jnp.inf), TOP_K)
    gate = jnp.take_along_axis(scores, top_e, axis=1)
    gate = gate / jnp.sum(gate, axis=-1, keepdims=True) * ROUTED_SCALE

    n_assign = t * TOP_K
    flat_e = top_e.reshape(-1)
    flat_tok = jnp.repeat(jnp.arange(t, dtype=jnp.int32), TOP_K)
    flat_w = gate.reshape(-1)
    order = jnp.argsort(flat_e)
    sorted_e = flat_e[order]
    counts = jnp.bincount(flat_e, length=N_EXPERTS)
    padded = (counts + MOE_BLK - 1) // MOE_BLK * MOE_BLK
    pad_end = jnp.cumsum(padded)
    pad_start = pad_end - padded
    start = jnp.cumsum(counts) - counts
    slot = pad_start[sorted_e] + jnp.arange(n_assign) - start[sorted_e]
    n_blocks = -(-n_assign // MOE_BLK) + N_EXPERTS
    n_slots = n_blocks * MOE_BLK
    slot_tok = jnp.zeros((n_slots,), jnp.int32).at[slot].set(flat_tok[order])
    slot_w = jnp.zeros((n_slots,), jnp.float32).at[slot].set(flat_w[order])
    block_e = jnp.minimum(
        jnp.searchsorted(pad_end, jnp.arange(n_blocks) * MOE_BLK, side='right'), N_EXPERTS - 1)

    def run_block(args):
        tok, wb, e = args
        xb = h[tok]
        yb = (jax.nn.silu(xb @ w_gate[e]) * (xb @ w_up[e])) @ w_down[e]
        return yb * wb[:, None].astype(yb.dtype)

    y = lax.map(run_block, (slot_tok.reshape(n_blocks, MOE_BLK),
                            slot_w.reshape(n_blocks, MOE_BLK), block_e)).reshape(n_slots, d)
    routed = jax.ops.segment_sum(y, slot_tok, num_segments=t)
    shared = (jax.nn.silu(h @ ws_gate) * (h @ ws_up)) @ ws_down
    return routed + shared


def setup_inputs(seed: int = 0) -> dict:
    key = jax.random.key(seed)
    ks = jax.random.split(key, 24)
    d, e, f = D_MODEL, N_EXPERTS, EXPERT_DIM
    n_da = DEPTH // N_MIXERS

    def normal(k, shape, scale=1.0):
        return jax.random.normal(k, shape, jnp.float32) * scale

    return {
        'x': normal(ks[0], (BATCH, SEQ, d)),
        'c': normal(ks[1], (BATCH, d)),
        'w_mod': normal(ks[2], (DEPTH, d, 6 * d), 0.5 * d ** -0.5),
        'b_mod': normal(ks[3], (DEPTH, 6 * d), 0.02),
        'norm_mix': 1.0 + normal(ks[4], (DEPTH, d), 0.02),
        'norm_ffn': 1.0 + normal(ks[5], (DEPTH, d), 0.02),
        'w_in': normal(ks[6], (DEPTH, d, 3 * d), d ** -0.5),
        'w_out': normal(ks[7], (DEPTH, d, d), d ** -0.5),
        'q_norm': 1.0 + normal(ks[8], (n_da, DA_QK_DIM), 0.02),
        'k_norm': 1.0 + normal(ks[9], (n_da, DA_QK_DIM), 0.02),
        'lambda_q1': normal(ks[10], (n_da, DA_QK_DIM), 0.1),
        'lambda_k1': normal(ks[11], (n_da, DA_QK_DIM), 0.1),
        'lambda_q2': normal(ks[12], (n_da, DA_QK_DIM), 0.1),
        'lambda_k2': normal(ks[13], (n_da, DA_QK_DIM), 0.1),
        'subln': 1.0 + normal(ks[14], (n_da, DA_V_DIM), 0.02),
        'w_router': normal(ks[15], (DEPTH, e, d), d ** -0.5),
        'router_bias': normal(ks[16], (DEPTH, e), 0.01),
        'w_gate': normal(ks[17], (DEPTH, e, d, f), d ** -0.5),
        'w_up': normal(ks[18], (DEPTH, e, d, f), d ** -0.5),
        'w_down': normal(ks[19], (DEPTH, e, f, d), f ** -0.5),
        'ws_gate': normal(ks[20], (DEPTH, d, SHARED_DIM), d ** -0.5),
        'ws_up': normal(ks[21], (DEPTH, d, SHARED_DIM), d ** -0.5),
        'ws_down': normal(ks[22], (DEPTH, SHARED_DIM, d), SHARED_DIM ** -0.5),
    }


def reference(x, c, w_mod, b_mod, norm_mix, norm_ffn, w_in, w_out, q_norm, k_norm,
              lambda_q1, lambda_k1, lambda_q2, lambda_k2, subln, w_router, router_bias,
              w_gate, w_up, w_down, ws_gate, ws_up, ws_down):
    b, s, d = x.shape
    cond = jax.nn.silu(c)
    for i in range(DEPTH):
        mod = cond @ w_mod[i] + b_mod[i]
        sh1, sc1, g1, sh2, sc2, g2 = jnp.split(mod[:, None, :], 6, axis=-1)
        h = rms_norm(x, norm_mix[i]) * (1 + sc1) + sh1
        q, k, v = jnp.split(h @ w_in[i], 3, axis=-1)
        if i % N_MIXERS == 0:
            def heads(t):
                return t.reshape(b, s, SB_HEADS, SB_HEAD_DIM).transpose(0, 2, 1, 3)
            o = stick_breaking_attention(heads(q), heads(k), heads(v))
        else:
            j = i // N_MIXERS
            lambda_init = 0.8 - 0.6 * math.exp(-0.3 * i)
            qh = rms_norm(q.reshape(b, s, DA_HEADS, 2, DA_QK_DIM), q_norm[j]).transpose(0, 2, 1, 3, 4)
            kh = rms_norm(k.reshape(b, s, DA_HEADS, 2, DA_QK_DIM), k_norm[j]).transpose(0, 2, 1, 3, 4)
            vh = v.reshape(b, s, DA_HEADS, DA_V_DIM).transpose(0, 2, 1, 3)
            lam = (jnp.exp(jnp.sum(lambda_q1[j].astype(jnp.float32) * lambda_k1[j].astype(jnp.float32)))
                   - jnp.exp(jnp.sum(lambda_q2[j].astype(jnp.float32) * lambda_k2[j].astype(jnp.float32)))
                   + lambda_init)
            o = differential_attention(qh, kh, vh, lam)
            o = rms_norm(o, subln[j], SUBLN_EPS) * (1.0 - lambda_init)
        o = o.transpose(0, 2, 1, 3).reshape(b, s, d)
        x = x + g1 * (o @ w_out[i])
        h = rms_norm(x, norm_ffn[i]) * (1 + sc2) + sh2
        y = moe_ffn(h.reshape(b * s, d), w_router[i], router_bias[i], w_gate[i], w_up[i],
                    w_down[i], ws_gate[i], ws_up[i], ws_down[i])
        x = x + g2 * y.reshape(b, s, d)
    return x
```

```python
import functools
import math

import jax
import jax.numpy as jnp
from jax import lax
from jax.experimental import pallas as pl
from jax.experimental.pallas import tpu as pltpu

F32 = jnp.float32
BF16 = jnp.bfloat16
I32 = jnp.int32

LANES = 128
SUBLANES = 8
VMEM_LIMIT = 56 * 1024 * 1024

SB_HEADS = 16
DA_HEADS = 8
HEAD_DIM = 128
N_EXPERTS = 64
TOP_K = 6
N_GROUPS = 8
TOPK_GROUPS = 4
ROUTED_SCALE = 2.5
EPS = 1e-6
SUBLN_EPS = 1e-5

ATT_TQ = 256
ATT_TK = 256
PROJ_TM = 512
ROUTE_TM = 512
MOE_BLK = 256
COMB_TM = 128


def _cparams(sem):
    return pltpu.CompilerParams(dimension_semantics=sem, vmem_limit_bytes=VMEM_LIMIT)


def _silu(x):
    return x * jax.nn.sigmoid(x)


def _dot(a, b):
    return jnp.dot(a, b, preferred_element_type=F32)


def _dot_nt(a, b):
    return lax.dot_general(a, b, (((1,), (1,)), ((), ())), preferred_element_type=F32)


def _mod_kernel(c_ref, w_ref, b_ref, o_ref):
    cond = _silu(c_ref[...])
    o_ref[0] = _dot(cond.astype(BF16), w_ref[0].astype(BF16)) + b_ref[0]


def _modulation(c, w_mod, b_mod):
    depth, d, n = w_mod.shape
    b = c.shape[0]
    rows = -(-b // SUBLANES) * SUBLANES
    cp = jnp.pad(c, ((0, rows - b), (0, 0)))
    tn = 1024
    out = pl.pallas_call(
        _mod_kernel,
        grid=(depth, n // tn),
        in_specs=[
            pl.BlockSpec((rows, d), lambda l, j: (0, 0)),
            pl.BlockSpec((1, d, tn), lambda l, j: (l, 0, j)),
            pl.BlockSpec((1, 1, tn), lambda l, j: (l, 0, j)),
        ],
        out_specs=pl.BlockSpec((1, rows, tn), lambda l, j: (l, 0, j)),
        out_shape=jax.ShapeDtypeStruct((depth, rows, n), F32),
        compiler_params=_cparams(("arbitrary", "arbitrary")),
        name="adaln_mod",
    )(cp, w_mod, b_mod.reshape(depth, 1, n))
    return out[:, :b]


def _norm_mod(x, g, sc, sh):
    ms = jnp.mean(x * x, axis=-1, keepdims=True)
    return x * lax.rsqrt(ms + EPS) * g * (1.0 + sc) + sh


def _qkv_kernel(x_ref, g_ref, sc_ref, sh_ref, w_ref, cs_ref, qk_ref, v_ref, h_ref, *, qk_norm, v_width):
    j = pl.program_id(2)

    @pl.when(j == 0)
    def _():
        h_ref[...] = _norm_mod(x_ref[0], g_ref[...], sc_ref[0], sh_ref[0]).astype(BF16)

    acc = _dot(h_ref[...], w_ref[...])
    n_chunks = acc.shape[1] // HEAD_DIM

    @pl.when(j < 2)
    def _():
        for c in range(n_chunks):
            a = acc[:, c * HEAD_DIM:(c + 1) * HEAD_DIM]
            if qk_norm:
                a = a * lax.rsqrt(jnp.mean(a * a, axis=-1, keepdims=True) + EPS)
            a = a * cs_ref[0][:, c * HEAD_DIM:(c + 1) * HEAD_DIM]
            qk_ref[0, c] = a.astype(BF16)

    @pl.when(j == 2)
    def _():
        for c in range(acc.shape[1] // v_width):
            v_ref[0, c] = acc[:, c * v_width:(c + 1) * v_width].astype(BF16)


def _qkv_proj(x, g, sc, sh, w_bf16, colscale, *, qk_norm, v_width):
    b, s, d = x.shape
    tm = PROJ_TM
    n_ch = d // HEAD_DIM
    kern = functools.partial(_qkv_kernel, qk_norm=qk_norm, v_width=v_width)
    return pl.pallas_call(
        kern,
        grid=(b, s // tm, 3),
        in_specs=[
            pl.BlockSpec((1, tm, d), lambda bi, i, j: (bi, i, 0)),
            pl.BlockSpec((1, d), lambda bi, i, j: (0, 0)),
            pl.BlockSpec((1, 1, d), lambda bi, i, j: (bi, 0, 0)),
            pl.BlockSpec((1, 1, d), lambda bi, i, j: (bi, 0, 0)),
            pl.BlockSpec((d, d), lambda bi, i, j: (0, j)),
            pl.BlockSpec((1, 1, d), lambda bi, i, j: (j, 0, 0)),
        ],
        out_specs=[
            pl.BlockSpec((1, n_ch, tm, HEAD_DIM), lambda bi, i, j: (bi, jnp.minimum(j, 1), i, 0)),
            pl.BlockSpec((1, d // v_width, tm, v_width), lambda bi, i, j: (bi, 0, i, 0)),
        ],
        out_shape=[
            jax.ShapeDtypeStruct((b, 2 * n_ch, s, HEAD_DIM), BF16),
            jax.ShapeDtypeStruct((b, d // v_width, s, v_width), BF16),
        ],
        scratch_shapes=[pltpu.VMEM((tm, d), BF16)],
        compiler_params=_cparams(("arbitrary", "arbitrary", "arbitrary")),
        name="norm_qkv_proj",
    )(x, g.reshape(1, d), sc, sh, w_bf16, colscale)


def _sb_kernel(q_ref, k_ref, v_ref, u_ref, o_ref, *, tq, tk):
    i = pl.program_id(2)
    q = q_ref[0, 0]
    n_chunks = ((i + 1) * tq) // tk
    rows = i * tq + lax.broadcasted_iota(I32, (tq, 1), 0)
    u = u_ref[...]

    def body(step, carry):
        acc, run = carry
        c = n_chunks - 1 - step
        start = pl.multiple_of(c * tk, tk)
        k = k_ref[0, 0, pl.ds(start, tk), :]
        v = v_ref[0, 0, pl.ds(start, tk), :]
        z = _dot_nt(q, k)
        cols = start + lax.broadcasted_iota(I32, (1, tk), 1)
        earlier = cols < rows
        softplus = jnp.maximum(z, 0.0) + jnp.log(1.0 + jnp.exp(-jnp.abs(z)))
        log_keep = jnp.where(earlier, -softplus, 0.0)
        hi = log_keep.astype(BF16)
        lo = (log_keep - hi.astype(F32)).astype(BF16)
        between = _dot(hi, u) + _dot(lo, u) + run
        a = jnp.where(earlier, jnp.exp(z + log_keep + between), 0.0)
        acc = acc + _dot(a.astype(BF16), v)
        run = run + jnp.sum(log_keep, axis=-1, keepdims=True)
        return acc, run

    acc0 = jnp.zeros((tq, HEAD_DIM), F32)
    run0 = jnp.zeros((tq, 1), F32)
    acc, _ = lax.fori_loop(0, n_chunks, body, (acc0, run0))
    o_ref[0] = acc.astype(BF16)


def _sb_attention(qk, v, d):
    b, _, s, _ = qk.shape
    tq, tk = ATT_TQ, ATT_TK
    heads = SB_HEADS
    jj = lax.broadcasted_iota(I32, (tk, tk), 0)
    ss = lax.broadcasted_iota(I32, (tk, tk), 1)
    u = (jj > ss).astype(BF16)
    kern = functools.partial(_sb_kernel, tq=tq, tk=tk)
    return pl.pallas_call(
        kern,
        grid=(b, heads, s // tq),
        in_specs=[
            pl.BlockSpec((1, 1, tq, HEAD_DIM), lambda bi, h, i: (bi, h, i, 0)),
            pl.BlockSpec((1, 1, s, HEAD_DIM), lambda bi, h, i: (bi, heads + h, 0, 0)),
            pl.BlockSpec((1, 1, s, HEAD_DIM), lambda bi, h, i: (bi, h, 0, 0)),
            pl.BlockSpec((tk, tk), lambda bi, h, i: (0, 0)),
        ],
        out_specs=pl.BlockSpec((1, tq, HEAD_DIM), lambda bi, h, i: (bi, i, h)),
        out_shape=jax.ShapeDtypeStruct((b, s, d), BF16),
        compiler_params=_cparams(("arbitrary", "arbitrary", "arbitrary")),
        name="stickbreak_attn",
    )(qk, qk, v, u)


def _da_kernel(q_ref, k_ref, v_ref, slope_ref, lam_ref, subln_ref, o_ref, *, tq, tk, lambda_init):
    i = pl.program_id(2)
    n_chunks = ((i + 1) * tq + tk - 1) // tk
    rows = i * tq + lax.broadcasted_iota(I32, (tq, 1), 0)
    slope = slope_ref[0]
    qs = (q_ref[0, 0], q_ref[0, 1])
    dv = v_ref.shape[-1]

    def body(c, carry):
        start = pl.multiple_of(c * tk, tk)
        cols = start + lax.broadcasted_iota(I32, (1, tk), 1)
        dist = rows - cols
        valid = dist >= 0
        bias = slope * dist.astype(F32)
        v = v_ref[0, 0, pl.ds(start, tk), :]
        new = []
        for m in range(2):
            mx, den, acc = carry[3 * m:3 * m + 3]
            k = k_ref[0, m, pl.ds(start, tk), :]
            z = jnp.where(valid, _dot_nt(qs[m], k) - bias, -jnp.inf)
            mx_new = jnp.maximum(mx, jnp.max(z, axis=-1, keepdims=True))
            alpha = jnp.exp(mx - mx_new)
            p = jnp.exp(z - mx_new)
            den = alpha * den + jnp.sum(p, axis=-1, keepdims=True)
            acc = alpha * acc + _dot(p.astype(BF16), v)
            new += [mx_new, den, acc]
        return tuple(new)

    init = []
    for _ in range(2):
        init += [jnp.full((tq, 1), -jnp.inf, F32), jnp.zeros((tq, 1), F32), jnp.zeros((tq, dv), F32)]
    m1, l1, a1, m2, l2, a2 = lax.fori_loop(0, n_chunks, body, tuple(init))

    lv = lam_ref[...]
    s1 = jnp.sum(lv[0:1] * lv[1:2], axis=-1, keepdims=True)
    s2 = jnp.sum(lv[2:3] * lv[3:4], axis=-1, keepdims=True)
    lam = jnp.exp(s1) - jnp.exp(s2) + lambda_init
    o = a1 / l1 - lam * (a2 / l2)
    o = o * lax.rsqrt(jnp.mean(o * o, axis=-1, keepdims=True) + SUBLN_EPS) * subln_ref[...]
    o_ref[0] = (o * (1.0 - lambda_init)).astype(BF16)


def _da_attention(qk, v, lam_rows, subln, d, lambda_init):
    b, _, s, _ = qk.shape
    tq, tk = ATT_TQ, ATT_TK
    heads = DA_HEADS
    dv = v.shape[-1]
    slopes = jnp.asarray(
        [2.0 ** (-8.0 * (h + 1) / heads) for h in range(heads)], F32).reshape(heads, 1, 1)
    kern = functools.partial(_da_kernel, tq=tq, tk=tk, lambda_init=lambda_init)
    return pl.pallas_call(
        kern,
        grid=(b, heads, s // tq),
        in_specs=[
            pl.BlockSpec((1, 2, tq, HEAD_DIM), lambda bi, h, i: (bi, h, i, 0)),
            pl.BlockSpec((1, 2, s, HEAD_DIM), lambda bi, h, i: (bi, heads + h, 0, 0)),
            pl.BlockSpec((1, 1, s, dv), lambda bi, h, i: (bi, h, 0, 0)),
            pl.BlockSpec((1, 1, 1), lambda bi, h, i: (h, 0, 0)),
            pl.BlockSpec((4, HEAD_DIM), lambda bi, h, i: (0, 0)),
            pl.BlockSpec((1, dv), lambda bi, h, i: (0, 0)),
        ],
        out_specs=pl.BlockSpec((1, tq, dv), lambda bi, h, i: (bi, i, h)),
        out_shape=jax.ShapeDtypeStruct((b, s, d), BF16),
        compiler_params=_cparams(("arbitrary", "arbitrary", "arbitrary")),
        name="diff_attn",
    )(qk, qk, v, slopes, lam_rows, subln.reshape(1, dv))


def _outproj_kernel(o_ref, w_ref, x_ref, g_ref, y_ref):
    y_ref[0] = x_ref[0] + g_ref[0] * _dot(o_ref[0], w_ref[...])


def _out_proj(o, w_bf16, x, gate):
    b, s, d = x.shape
    tm = PROJ_TM
    return pl.pallas_call(
        _outproj_kernel,
        grid=(b, s // tm),
        in_specs=[
            pl.BlockSpec((1, tm, d), lambda bi, i: (bi, i, 0)),
            pl.BlockSpec((d, d), lambda bi, i: (0, 0)),
            pl.BlockSpec((1, tm, d), lambda bi, i: (bi, i, 0)),
            pl.BlockSpec((1, 1, d), lambda bi, i: (bi, 0, 0)),
        ],
        out_specs=pl.BlockSpec((1, tm, d), lambda bi, i: (bi, i, 0)),
        out_shape=jax.ShapeDtypeStruct((b, s, d), F32),
        compiler_params=_cparams(("arbitrary", "arbitrary")),
        name="out_proj_residual",
    )(o, w_bf16, x, gate)


def _router_kernel(x_ref, g_ref, sc_ref, sh_ref, wr_ref, bias_ref, u_ref,
                   h_ref, te_ref, gate_ref, rank_ref, cnt_ref, run_ref, *, tm):
    i = pl.program_id(0)
    per_group = N_EXPERTS // N_GROUPS

    @pl.when(i == 0)
    def _():
        run_ref[...] = jnp.zeros_like(run_ref)

    h = _norm_mod(x_ref[...], g_ref[...], sc_ref[0], sh_ref[0])
    h_ref[...] = h
    logits = lax.dot_general(wr_ref[...], h, (((1,), (1,)), ((), ())),
                             precision=lax.Precision.HIGHEST, preferred_element_type=F32)
    scores = jax.nn.sigmoid(logits)
    biased = scores + bias_ref[...]
    neg = -jnp.inf
    sub = lax.broadcasted_iota(I32, (per_group, tm), 0)

    def first_max(vals, idx, sentinel):
        m = jnp.max(vals, axis=0, keepdims=True)
        return m, jnp.min(jnp.where(vals == m, idx, sentinel), axis=0, keepdims=True)

    sc_g = [scores[g * per_group:(g + 1) * per_group] for g in range(N_GROUPS)]
    bi_g = [biased[g * per_group:(g + 1) * per_group] for g in range(N_GROUPS)]

    gscore = jnp.zeros((N_GROUPS, tm), F32)
    for g in range(N_GROUPS):
        m1, i1 = first_max(bi_g[g], sub, per_group)
        m2 = jnp.max(jnp.where(sub == i1, neg, bi_g[g]), axis=0, keepdims=True)
        gscore = jnp.where(sub == g, m1 + m2, gscore)

    keep = jnp.zeros((N_GROUPS, tm), F32)
    cur = gscore
    for _ in range(TOPK_GROUPS):
        _, gi = first_max(cur, sub, N_GROUPS)
        hit = sub == gi
        keep = jnp.where(hit, 1.0, keep)
        cur = jnp.where(hit, neg, cur)

    masked = []
    for g in range(N_GROUPS):
        kg = jnp.max(jnp.where(sub == g, keep, 0.0), axis=0, keepdims=True)
        masked.append(jnp.where(kg > 0.0, bi_g[g], neg))
    eidx = [sub + g * per_group for g in range(N_GROUPS)]
    sel = [jnp.zeros((per_group, tm), F32) for _ in range(N_GROUPS)]

    top_e, gates = [], []
    for _ in range(TOP_K):
        m = masked[0].max(axis=0, keepdims=True)
        for g in range(1, N_GROUPS):
            m = jnp.maximum(m, jnp.max(masked[g], axis=0, keepdims=True))
        ei = jnp.full((1, tm), N_EXPERTS, I32)
        for g in range(N_GROUPS):
            ei = jnp.minimum(ei, jnp.min(jnp.where(masked[g] == m, eidx[g], N_EXPERTS), axis=0, keepdims=True))
        gt = jnp.zeros((1, tm), F32)
        for g in range(N_GROUPS):
            hit = eidx[g] == ei
            gt = gt + jnp.sum(jnp.where(hit, sc_g[g], 0.0), axis=0, keepdims=True)
            masked[g] = jnp.where(hit, neg, masked[g])
            sel[g] = jnp.where(hit, 1.0, sel[g])
        top_e.append(ei)
        gates.append(gt)

    gsum = gates[0]
    for gt in gates[1:]:
        gsum = gsum + gt

    sel_all = jnp.concatenate(sel, axis=0)
    before = _dot(sel_all.astype(BF16), u_ref[...]) + run_ref[...]
    run_new = run_ref[...] + jnp.sum(sel_all, axis=1, keepdims=True)
    run_ref[...] = run_new
    cnt_ref[...] = jnp.broadcast_to(run_new, cnt_ref.shape).astype(I32)

    te_ref[...] = jnp.zeros_like(te_ref)
    gate_ref[...] = jnp.zeros_like(gate_ref)
    rank_ref[...] = jnp.zeros_like(rank_ref)
    for j in range(TOP_K):
        rk = jnp.zeros((1, tm), F32)
        for g in range(N_GROUPS):
            rk = rk + jnp.sum(jnp.where(eidx[g] == top_e[j], before[g * per_group:(g + 1) * per_group], 0.0),
                              axis=0, keepdims=True)
        te_ref[j:j + 1, :] = top_e[j]
        gate_ref[j:j + 1, :] = gates[j] / gsum * ROUTED_SCALE
        rank_ref[j:j + 1, :] = rk.astype(I32)


def _route(x2, g, sc, sh, w_router, bias, seq):
    t, d = x2.shape
    tm = ROUTE_TM
    e = N_EXPERTS
    jj = lax.broadcasted_iota(I32, (tm, tm), 0)
    ss = lax.broadcasted_iota(I32, (tm, tm), 1)
    u = (jj < ss).astype(BF16)
    per_seq = seq // tm
    kern = functools.partial(_router_kernel, tm=tm)
    rows = SUBLANES
    h, te, gate, rank, cnt = pl.pallas_call(
        kern,
        grid=(t // tm,),
        in_specs=[
            pl.BlockSpec((tm, d), lambda i: (i, 0)),
            pl.BlockSpec((1, d), lambda i: (0, 0)),
            pl.BlockSpec((1, 1, d), lambda i: (i // per_seq, 0, 0)),
            pl.BlockSpec((1, 1, d), lambda i: (i // per_seq, 0, 0)),
            pl.BlockSpec((e, d), lambda i: (0, 0)),
            pl.BlockSpec((e, 1), lambda i: (0, 0)),
            pl.BlockSpec((tm, tm), lambda i: (0, 0)),
        ],
        out_specs=[
            pl.BlockSpec((tm, d), lambda i: (i, 0)),
            pl.BlockSpec((rows, tm), lambda i: (0, i)),
            pl.BlockSpec((rows, tm), lambda i: (0, i)),
            pl.BlockSpec((rows, tm), lambda i: (0, i)),
            pl.BlockSpec((e, LANES), lambda i: (0, 0)),
        ],
        out_shape=[
            jax.ShapeDtypeStruct((t, d), F32),
            jax.ShapeDtypeStruct((rows, t), I32),
            jax.ShapeDtypeStruct((rows, t), F32),
            jax.ShapeDtypeStruct((rows, t), I32),
            jax.ShapeDtypeStruct((e, LANES), I32),
        ],
        scratch_shapes=[pltpu.VMEM((e, 1), F32)],
        compiler_params=_cparams(("arbitrary",)),
        name="norm_router_topk",
    )(x2, g.reshape(1, d), sc, sh, w_router, bias.reshape(e, 1), u)
    return h, te[:TOP_K], gate[:TOP_K], rank[:TOP_K], cnt[:, 0]


def _expert_kernel(be_ref, nu_ref, tok_ref, h_hbm, wg_ref, wu_ref, wd_ref, sw_ref, y_ref,
                   xbuf, wgb, wub, wdb, sem, *, blk):
    i = pl.program_id(0)
    n_used = nu_ref[0]

    def issue(block, slot):
        base = block * blk

        def body(r, carry):
            tok = tok_ref[base + r]
            pltpu.make_async_copy(h_hbm.at[pl.ds(tok, 1)], xbuf.at[slot, pl.ds(r, 1)], sem.at[slot]).start()
            return carry

        lax.fori_loop(0, blk, body, 0, unroll=8)

    @pl.when(i == 0)
    def _():
        issue(0, 0)

    @pl.when(i + 1 < n_used)
    def _():
        issue(i + 1, (i + 1) % 2)

    @pl.when(i < n_used)
    def _():
        slot = i % 2
        pltpu.make_async_copy(h_hbm.at[pl.ds(0, blk)], xbuf.at[slot], sem.at[slot]).wait()
        prev = be_ref[jnp.maximum(i - 1, 0)]

        @pl.when((i == 0) | (be_ref[i] != prev))
        def _():
            wgb[...] = wg_ref[0].astype(BF16)
            wub[...] = wu_ref[0].astype(BF16)
            wdb[...] = wd_ref[0].astype(BF16)

        x = xbuf[slot].astype(BF16)
        mid = _silu(_dot(x, wgb[...])) * _dot(x, wub[...])
        y_ref[...] = _dot(mid.astype(BF16), wdb[...]) * sw_ref[...]

    @pl.when(i >= n_used)
    def _():
        y_ref[...] = jnp.zeros_like(y_ref)


def _experts(h, w_gate, w_up, w_down, block_e, n_used, slot_tok, slot_w):
    t, d = h.shape
    e, _, f = w_gate.shape
    blk = MOE_BLK
    n_blocks = block_e.shape[0]
    kern = functools.partial(_expert_kernel, blk=blk)
    grid_spec = pltpu.PrefetchScalarGridSpec(
        num_scalar_prefetch=3,
        grid=(n_blocks,),
        in_specs=[
            pl.BlockSpec(memory_space=pl.ANY),
            pl.BlockSpec((1, d, f), lambda i, be, nu, tk: (be[i], 0, 0)),
            pl.BlockSpec((1, d, f), lambda i, be, nu, tk: (be[i], 0, 0)),
            pl.BlockSpec((1, f, d), lambda i, be, nu, tk: (be[i], 0, 0)),
            pl.BlockSpec((blk, 1), lambda i, be, nu, tk: (i, 0)),
        ],
        out_specs=pl.BlockSpec((blk, d), lambda i, be, nu, tk: (i, 0)),
        scratch_shapes=[
            pltpu.VMEM((2, blk, d), F32),
            pltpu.VMEM((d, f), BF16),
            pltpu.VMEM((d, f), BF16),
            pltpu.VMEM((f, d), BF16),
            pltpu.SemaphoreType.DMA((2,)),
        ],
    )
    return pl.pallas_call(
        kern,
        grid_spec=grid_spec,
        out_shape=jax.ShapeDtypeStruct((n_blocks * blk, d), F32),
        compiler_params=_cparams(("arbitrary",)),
        name="routed_experts",
    )(block_e, n_used, slot_tok, h, w_gate, w_up, w_down, slot_w.reshape(-1, 1))


def _combine_kernel(slot_ref, y_hbm, h_ref, x_ref, g_ref, wg_ref, wu_ref, wd_ref, o_ref, ybuf, sem, *, tm):
    i = pl.program_id(0)
    n = pl.num_programs(0)
    rows = TOP_K * tm

    def issue(tile, slot):
        base = tile * rows

        def body(r, carry):
            src = slot_ref[base + r]
            pltpu.make_async_copy(y_hbm.at[pl.ds(src, 1)], ybuf.at[slot, pl.ds(r, 1)], sem.at[slot]).start()
            return carry

        lax.fori_loop(0, rows, body, 0, unroll=8)

    @pl.when(i == 0)
    def _():
        issue(0, 0)

    @pl.when(i + 1 < n)
    def _():
        issue(i + 1, (i + 1) % 2)

    hb = h_ref[...].astype(BF16)
    mid = _silu(_dot(hb, wg_ref[...])) * _dot(hb, wu_ref[...])
    total = _dot(mid.astype(BF16), wd_ref[...])

    slot = i % 2
    pltpu.make_async_copy(y_hbm.at[pl.ds(0, rows)], ybuf.at[slot], sem.at[slot]).wait()
    for j in range(TOP_K):
        total = total + ybuf[slot, j * tm:(j + 1) * tm, :]
    o_ref[...] = x_ref[...] + g_ref[0] * total


def _combine(slots_tiled, y_slots, h, x2, gate, ws_gate, ws_up, ws_down, seq):
    t, d = h.shape
    f = ws_gate.shape[1]
    tm = COMB_TM
    per_seq = seq // tm
    kern = functools.partial(_combine_kernel, tm=tm)
    grid_spec = pltpu.PrefetchScalarGridSpec(
        num_scalar_prefetch=1,
        grid=(t // tm,),
        in_specs=[
            pl.BlockSpec(memory_space=pl.ANY),
            pl.BlockSpec((tm, d), lambda i, sl: (i, 0)),
            pl.BlockSpec((tm, d), lambda i, sl: (i, 0)),
            pl.BlockSpec((1, 1, d), lambda i, sl: (i // per_seq, 0, 0)),
            pl.BlockSpec((d, f), lambda i, sl: (0, 0)),
            pl.BlockSpec((d, f), lambda i, sl: (0, 0)),
            pl.BlockSpec((f, d), lambda i, sl: (0, 0)),
        ],
        out_specs=pl.BlockSpec((tm, d), lambda i, sl: (i, 0)),
        scratch_shapes=[
            pltpu.VMEM((2, TOP_K * tm, d), F32),
            pltpu.SemaphoreType.DMA((2,)),
        ],
    )
    return pl.pallas_call(
        kern,
        grid_spec=grid_spec,
        out_shape=jax.ShapeDtypeStruct((t, d), F32),
        compiler_params=_cparams(("arbitrary",)),
        name="shared_expert_combine",
    )(slots_tiled, y_slots, h, x2, gate, ws_gate, ws_up, ws_down)


def _moe_layer(x, g, sc, sh, gate, w_router, router_bias, w_gate, w_up, w_down, ws_gate, ws_up, ws_down):
    b, s, d = x.shape
    t = b * s
    x2 = x.reshape(t, d)
    h, top_e, gates, rank, counts = _route(x2, g, sc, sh, w_router, router_bias, s)

    blk = MOE_BLK
    n_blocks = -(-(t * TOP_K) // blk) + N_EXPERTS
    padded = (counts + blk - 1) // blk * blk
    pad_end = jnp.cumsum(padded)
    pad_start = pad_end - padded
    slot = pad_start[top_e] + rank
    tok = jnp.broadcast_to(jnp.arange(t, dtype=I32)[None, :], slot.shape)
    slot_tok = jnp.zeros((n_blocks * blk,), I32).at[slot.reshape(-1)].set(tok.reshape(-1))
    slot_w = jnp.zeros((n_blocks * blk,), F32).at[slot.reshape(-1)].set(gates.reshape(-1))
    block_e = jnp.minimum(
        jnp.searchsorted(pad_end, jnp.arange(n_blocks, dtype=I32) * blk, side='right'), N_EXPERTS - 1).astype(I32)
    n_used = (pad_end[-1:] // blk).astype(I32)

    y_slots = _experts(h, w_gate, w_up, w_down, block_e, n_used, slot_tok, slot_w)

    tm = COMB_TM
    slots_tiled = slot.reshape(TOP_K, t // tm, tm).transpose(1, 0, 2).reshape(-1).astype(I32)
    out = _combine(slots_tiled, y_slots, h, x2, gate,
                   ws_gate.astype(BF16), ws_up.astype(BF16), ws_down.astype(BF16), s)
    return out.reshape(b, s, d)


def kernel(x, c, w_mod, b_mod, norm_mix, norm_ffn, w_in, w_out, q_norm, k_norm, lambda_q1, lambda_k1,
           lambda_q2, lambda_k2, subln, w_router, router_bias, w_gate, w_up, w_down, ws_gate, ws_up, ws_down):
    b, s, d = x.shape
    depth = w_mod.shape[0]
    assert d == SB_HEADS * HEAD_DIM == 2 * DA_HEADS * HEAD_DIM
    assert s % max(ATT_TQ, PROJ_TM, ROUTE_TM) == 0
    scale = HEAD_DIM ** -0.5

    mod = _modulation(c, w_mod, b_mod)
    for i in range(depth):
        sh1, sc1, g1, sh2, sc2, g2 = [m.reshape(b, 1, d) for m in jnp.split(mod[i], 6, axis=-1)]
        w_in_b = w_in[i].astype(BF16)
        if i % 2 == 0:
            colscale = jnp.stack([jnp.full((1, d), scale, F32), jnp.ones((1, d), F32), jnp.ones((1, d), F32)])
            qk, v = _qkv_proj(x, norm_mix[i], sc1, sh1, w_in_b, colscale, qk_norm=False, v_width=HEAD_DIM)
            o = _sb_attention(qk, v, d)
        else:
            j = i // 2
            lambda_init = 0.8 - 0.6 * math.exp(-0.3 * i)
            reps = d // HEAD_DIM
            colscale = jnp.stack([
                jnp.tile(q_norm[j].astype(F32), reps)[None, :] * scale,
                jnp.tile(k_norm[j].astype(F32), reps)[None, :],
                jnp.ones((1, d), F32)])
            qk, v = _qkv_proj(x, norm_mix[i], sc1, sh1, w_in_b, colscale, qk_norm=True, v_width=2 * HEAD_DIM)
            lam_rows = jnp.stack([lambda_q1[j], lambda_k1[j], lambda_q2[j], lambda_k2[j]]).astype(F32)
            o = _da_attention(qk, v, lam_rows, subln[j].astype(F32), d, lambda_init)
        x = _out_proj(o, w_out[i].astype(BF16), x, g1)
        x = _moe_layer(x, norm_ffn[i], sc2, sh2, g2, w_router[i], router_bias[i],
                       w_gate[i], w_up[i], w_down[i], ws_gate[i], ws_up[i], ws_down[i])
    return x
```

```python
import functools
import math

import jax
import jax.numpy as jnp
from jax import lax
from jax.experimental import pallas as pl
from jax.experimental.pallas import tpu as pltpu

F32 = jnp.float32
BF16 = jnp.bfloat16
I32 = jnp.int32

LANES = 128
SUBLANES = 8
VMEM_LIMIT = 56 * 1024 * 1024

SB_HEADS = 16
DA_HEADS = 8
HEAD_DIM = 128
N_EXPERTS = 64
TOP_K = 6
N_GROUPS = 8
TOPK_GROUPS = 4
ROUTED_SCALE = 2.5
EPS = 1e-6
SUBLN_EPS = 1e-5
LOG2E = math.log2(math.e)

ATT_TQ = 512
ATT_TK = 256
PROJ_TM = 512
ROUTE_TM = 512
MOE_BLK = 256
DISP_TM = 512
COMB_TM = 128


def _cparams(sem):
    return pltpu.CompilerParams(dimension_semantics=sem, vmem_limit_bytes=VMEM_LIMIT)


def _silu(x):
    return x * jax.nn.sigmoid(x)


def _dot(a, b):
    return jnp.dot(a, b, preferred_element_type=F32)


def _dot_nt(a, b):
    return lax.dot_general(a, b, (((1,), (1,)), ((), ())), preferred_element_type=F32)


def _mod_kernel(c_ref, w_ref, b_ref, o_ref):
    cond = _silu(c_ref[...])
    o_ref[0] = _dot(cond.astype(BF16), w_ref[0].astype(BF16)) + b_ref[0]


def _modulation(c, w_mod, b_mod):
    depth, d, n = w_mod.shape
    b = c.shape[0]
    rows = -(-b // SUBLANES) * SUBLANES
    cp = jnp.pad(c, ((0, rows - b), (0, 0)))
    tn = 1024
    out = pl.pallas_call(
        _mod_kernel,
        grid=(depth, n // tn),
        in_specs=[
            pl.BlockSpec((rows, d), lambda l, j: (0, 0)),
            pl.BlockSpec((1, d, tn), lambda l, j: (l, 0, j)),
            pl.BlockSpec((1, 1, tn), lambda l, j: (l, 0, j)),
        ],
        out_specs=pl.BlockSpec((1, rows, tn), lambda l, j: (l, 0, j)),
        out_shape=jax.ShapeDtypeStruct((depth, rows, n), F32),
        compiler_params=_cparams(("arbitrary", "arbitrary")),
        name="adaln_mod",
    )(cp, w_mod, b_mod.reshape(depth, 1, n))
    return out[:, :b]


def _norm_mod(x, g, sc, sh):
    ms = jnp.mean(x * x, axis=-1, keepdims=True)
    return x * lax.rsqrt(ms + EPS) * g * (1.0 + sc) + sh


def _qkv_kernel(x_ref, g_ref, sc_ref, sh_ref, w_ref, cs_ref, qk_ref, v_ref, h_ref, *, qk_norm, v_width):
    j = pl.program_id(2)

    @pl.when(j == 0)
    def _():
        h_ref[...] = _norm_mod(x_ref[0], g_ref[...], sc_ref[0], sh_ref[0]).astype(BF16)

    acc = _dot(h_ref[...], w_ref[...])
    n_chunks = acc.shape[1] // HEAD_DIM

    @pl.when(j < 2)
    def _():
        for c in range(n_chunks):
            a = acc[:, c * HEAD_DIM:(c + 1) * HEAD_DIM]
            if qk_norm:
                a = a * lax.rsqrt(jnp.mean(a * a, axis=-1, keepdims=True) + EPS)
            a = a * cs_ref[0][:, c * HEAD_DIM:(c + 1) * HEAD_DIM]
            qk_ref[0, c] = a.astype(BF16)

    @pl.when(j == 2)
    def _():
        for c in range(acc.shape[1] // v_width):
            v_ref[0, c] = acc[:, c * v_width:(c + 1) * v_width].astype(BF16)


def _qkv_proj(x, g, sc, sh, w_bf16, colscale, *, qk_norm, v_width):
    b, s, d = x.shape
    tm = PROJ_TM
    n_ch = d // HEAD_DIM
    kern = functools.partial(_qkv_kernel, qk_norm=qk_norm, v_width=v_width)
    return pl.pallas_call(
        kern,
        grid=(b, s // tm, 3),
        in_specs=[
            pl.BlockSpec((1, tm, d), lambda bi, i, j: (bi, i, 0)),
            pl.BlockSpec((1, d), lambda bi, i, j: (0, 0)),
            pl.BlockSpec((1, 1, d), lambda bi, i, j: (bi, 0, 0)),
            pl.BlockSpec((1, 1, d), lambda bi, i, j: (bi, 0, 0)),
            pl.BlockSpec((d, d), lambda bi, i, j: (0, j)),
            pl.BlockSpec((1, 1, d), lambda bi, i, j: (j, 0, 0)),
        ],
        out_specs=[
            pl.BlockSpec((1, n_ch, tm, HEAD_DIM), lambda bi, i, j: (bi, jnp.minimum(j, 1), i, 0)),
            pl.BlockSpec((1, d // v_width, tm, v_width), lambda bi, i, j: (bi, 0, i, 0)),
        ],
        out_shape=[
            jax.ShapeDtypeStruct((b, 2 * n_ch, s, HEAD_DIM), BF16),
            jax.ShapeDtypeStruct((b, d // v_width, s, v_width), BF16),
        ],
        scratch_shapes=[pltpu.VMEM((tm, d), BF16)],
        compiler_params=_cparams(("arbitrary", "arbitrary", "arbitrary")),
        name="norm_qkv_proj",
    )(x, g.reshape(1, d), sc, sh, w_bf16, colscale)


def _sb_kernel(q_ref, k_ref, v_ref, u_ref, o_ref, *, tq, tk):
    i = pl.program_id(2)
    q = q_ref[0, 0]
    per = tq // tk
    rows = i * tq + lax.broadcasted_iota(I32, (tq, 1), 0)
    u = u_ref[...]

    def chunk(c, acc, run, masked):
        start = pl.multiple_of(c * tk, tk)
        k = k_ref[0, 0, pl.ds(start, tk), :]
        v = v_ref[0, 0, pl.ds(start, tk), :]
        z = _dot_nt(q, k)
        nz = -z
        e = jnp.exp2(jnp.minimum(z, nz))
        log_keep = jnp.minimum(nz, 0.0) - jnp.log(1.0 + e) * LOG2E
        if masked:
            earlier = (start + lax.broadcasted_iota(I32, (1, tk), 1)) < rows
            log_keep = jnp.where(earlier, log_keep, 0.0)
        between = _dot(log_keep.astype(BF16), u) + run
        a = jnp.exp2((z + log_keep) + between)
        if masked:
            a = jnp.where(earlier, a, 0.0)
        acc = acc + _dot(a.astype(BF16), v)
        run = run + jnp.sum(log_keep, axis=-1, keepdims=True)
        return acc, run

    acc = jnp.zeros((tq, HEAD_DIM), F32)
    run = jnp.zeros((tq, 1), F32)
    for r in range(per):
        acc, run = chunk(per * i + (per - 1 - r), acc, run, True)

    def body(step, carry):
        acc, run = carry
        p = i - 1 - step
        for r in range(per):
            acc, run = chunk(per * p + (per - 1 - r), acc, run, False)
        return acc, run

    acc, _ = lax.fori_loop(0, i, body, (acc, run))
    o_ref[0] = acc.astype(BF16)


def _sb_attention(qk, v, d):
    b, _, s, _ = qk.shape
    tq, tk = ATT_TQ, ATT_TK
    heads = SB_HEADS
    jj = lax.broadcasted_iota(I32, (tk, tk), 0)
    ss = lax.broadcasted_iota(I32, (tk, tk), 1)
    u = (jj > ss).astype(BF16)
    kern = functools.partial(_sb_kernel, tq=tq, tk=tk)
    return pl.pallas_call(
        kern,
        grid=(b, heads, s // tq),
        in_specs=[
            pl.BlockSpec((1, 1, tq, HEAD_DIM), lambda bi, h, i: (bi, h, i, 0)),
            pl.BlockSpec((1, 1, s, HEAD_DIM), lambda bi, h, i: (bi, heads + h, 0, 0)),
            pl.BlockSpec((1, 1, s, HEAD_DIM), lambda bi, h, i: (bi, h, 0, 0)),
            pl.BlockSpec((tk, tk), lambda bi, h, i: (0, 0)),
        ],
        out_specs=pl.BlockSpec((1, tq, HEAD_DIM), lambda bi, h, i: (bi, i, h)),
        out_shape=jax.ShapeDtypeStruct((b, s, d), BF16),
        compiler_params=_cparams(("arbitrary", "arbitrary", "arbitrary")),
        name="stickbreak_attn",
    )(qk, qk, v, u)


def _da_kernel(q_ref, k_ref, v_ref, slope_ref, lam_ref, subln_ref, o_ref, *, tq, tk, lambda_init):
    i = pl.program_id(2)
    per = tq // tk
    row0 = i * tq
    rows = row0 + lax.broadcasted_iota(I32, (tq, 1), 0)
    slope = slope_ref[0]
    qs = (q_ref[0, 0], q_ref[0, 1])
    dv = v_ref.shape[-1]

    def chunk(c, carry, masked):
        start = pl.multiple_of(c * tk, tk)
        cols = start + lax.broadcasted_iota(I32, (1, tk), 1)
        key_bias = slope * (cols - row0).astype(F32)
        v = v_ref[0, 0, pl.ds(start, tk), :]
        new = []
        for m in range(2):
            mx, den, acc = carry[3 * m:3 * m + 3]
            k = k_ref[0, m, pl.ds(start, tk), :]
            z = _dot_nt(qs[m], k) + key_bias
            if masked:
                z = jnp.where(cols <= rows, z, -jnp.inf)
            mx_new = jnp.maximum(mx, jnp.max(z, axis=-1, keepdims=True))
            alpha = jnp.exp2(mx - mx_new)
            p = jnp.exp2(z - mx_new)
            den = alpha * den + jnp.sum(p, axis=-1, keepdims=True)
            acc = alpha * acc + _dot(p.astype(BF16), v)
            new += [mx_new, den, acc]
        return tuple(new)

    init = []
    for _ in range(2):
        init += [jnp.full((tq, 1), -jnp.inf, F32), jnp.zeros((tq, 1), F32), jnp.zeros((tq, dv), F32)]
    carry = lax.fori_loop(0, per * i, lambda c, cr: chunk(c, cr, False), tuple(init))
    for r in range(per):
        carry = chunk(per * i + r, carry, True)
    _, l1, a1, _, l2, a2 = carry

    lv = lam_ref[...]
    s1 = jnp.sum(lv[0:1] * lv[1:2], axis=-1, keepdims=True)
    s2 = jnp.sum(lv[2:3] * lv[3:4], axis=-1, keepdims=True)
    lam = jnp.exp(s1) - jnp.exp(s2) + lambda_init
    o = a1 / l1 - lam * (a2 / l2)
    o = o * lax.rsqrt(jnp.mean(o * o, axis=-1, keepdims=True) + SUBLN_EPS) * subln_ref[...]
    o_ref[0] = (o * (1.0 - lambda_init)).astype(BF16)


def _da_attention(qk, v, lam_rows, subln, d, lambda_init):
    b, _, s, _ = qk.shape
    tq, tk = ATT_TQ, ATT_TK
    heads = DA_HEADS
    dv = v.shape[-1]
    slopes = jnp.asarray(
        [2.0 ** (-8.0 * (h + 1) / heads) * LOG2E for h in range(heads)], F32).reshape(heads, 1, 1)
    kern = functools.partial(_da_kernel, tq=tq, tk=tk, lambda_init=lambda_init)
    return pl.pallas_call(
        kern,
        grid=(b, heads, s // tq),
        in_specs=[
            pl.BlockSpec((1, 2, tq, HEAD_DIM), lambda bi, h, i: (bi, h, i, 0)),
            pl.BlockSpec((1, 2, s, HEAD_DIM), lambda bi, h, i: (bi, heads + h, 0, 0)),
            pl.BlockSpec((1, 1, s, dv), lambda bi, h, i: (bi, h, 0, 0)),
            pl.BlockSpec((1, 1, 1), lambda bi, h, i: (h, 0, 0)),
            pl.BlockSpec((4, HEAD_DIM), lambda bi, h, i: (0, 0)),
            pl.BlockSpec((1, dv), lambda bi, h, i: (0, 0)),
        ],
        out_specs=pl.BlockSpec((1, tq, dv), lambda bi, h, i: (bi, i, h)),
        out_shape=jax.ShapeDtypeStruct((b, s, d), BF16),
        compiler_params=_cparams(("arbitrary", "arbitrary", "arbitrary")),
        name="diff_attn",
    )(qk, qk, v, slopes, lam_rows, subln.reshape(1, dv))


def _outproj_kernel(o_ref, w_ref, x_ref, g_ref, y_ref):
    y_ref[0] = x_ref[0] + g_ref[0] * _dot(o_ref[0], w_ref[...])


def _out_proj(o, w_bf16, x, gate):
    b, s, d = x.shape
    tm = PROJ_TM
    return pl.pallas_call(
        _outproj_kernel,
        grid=(b, s // tm),
        in_specs=[
            pl.BlockSpec((1, tm, d), lambda bi, i: (bi, i, 0)),
            pl.BlockSpec((d, d), lambda bi, i: (0, 0)),
            pl.BlockSpec((1, tm, d), lambda bi, i: (bi, i, 0)),
            pl.BlockSpec((1, 1, d), lambda bi, i: (bi, 0, 0)),
        ],
        out_specs=pl.BlockSpec((1, tm, d), lambda bi, i: (bi, i, 0)),
        out_shape=jax.ShapeDtypeStruct((b, s, d), F32),
        compiler_params=_cparams(("arbitrary", "arbitrary")),
        name="out_proj_residual",
    )(o, w_bf16, x, gate)


def _router_kernel(x_ref, g_ref, sc_ref, sh_ref, wr_ref, bias_ref, u_ref,
                   h_ref, te_ref, gate_ref, rank_ref, cnt_ref, run_ref, *, tm):
    i = pl.program_id(0)
    per_group = N_EXPERTS // N_GROUPS

    @pl.when(i == 0)
    def _():
        run_ref[...] = jnp.zeros_like(run_ref)

    h = _norm_mod(x_ref[...], g_ref[...], sc_ref[0], sh_ref[0])
    h_ref[...] = h
    logits = lax.dot_general(wr_ref[0], h, (((1,), (1,)), ((), ())),
                             precision=lax.Precision.HIGHEST, preferred_element_type=F32)
    scores = jax.nn.sigmoid(logits)
    biased = scores + bias_ref[0]
    neg = -jnp.inf
    sub = lax.broadcasted_iota(I32, (per_group, tm), 0)

    def first_max(vals, idx, sentinel):
        m = jnp.max(vals, axis=0, keepdims=True)
        return m, jnp.min(jnp.where(vals == m, idx, sentinel), axis=0, keepdims=True)

    sc_g = [scores[g * per_group:(g + 1) * per_group] for g in range(N_GROUPS)]
    bi_g = [biased[g * per_group:(g + 1) * per_group] for g in range(N_GROUPS)]

    gscore = jnp.zeros((N_GROUPS, tm), F32)
    for g in range(N_GROUPS):
        m1, i1 = first_max(bi_g[g], sub, per_group)
        m2 = jnp.max(jnp.where(sub == i1, neg, bi_g[g]), axis=0, keepdims=True)
        gscore = jnp.where(sub == g, m1 + m2, gscore)

    keep = jnp.zeros((N_GROUPS, tm), F32)
    cur = gscore
    for _ in range(TOPK_GROUPS):
        _, gi = first_max(cur, sub, N_GROUPS)
        hit = sub == gi
        keep = jnp.where(hit, 1.0, keep)
        cur = jnp.where(hit, neg, cur)

    masked = []
    for g in range(N_GROUPS):
        kg = jnp.max(jnp.where(sub == g, keep, 0.0), axis=0, keepdims=True)
        masked.append(jnp.where(kg > 0.0, bi_g[g], neg))
    eidx = [sub + g * per_group for g in range(N_GROUPS)]
    sel = [jnp.zeros((per_group, tm), F32) for _ in range(N_GROUPS)]

    top_e, gates = [], []
    for _ in range(TOP_K):
        m = masked[0].max(axis=0, keepdims=True)
        for g in range(1, N_GROUPS):
            m = jnp.maximum(m, jnp.max(masked[g], axis=0, keepdims=True))
        ei = jnp.full((1, tm), N_EXPERTS, I32)
        for g in range(N_GROUPS):
            ei = jnp.minimum(ei, jnp.min(jnp.where(masked[g] == m, eidx[g], N_EXPERTS), axis=0, keepdims=True))
        gt = jnp.zeros((1, tm), F32)
        for g in range(N_GROUPS):
            hit = eidx[g] == ei
            gt = gt + jnp.sum(jnp.where(hit, sc_g[g], 0.0), axis=0, keepdims=True)
            masked[g] = jnp.where(hit, neg, masked[g])
            sel[g] = jnp.where(hit, 1.0, sel[g])
        top_e.append(ei)
        gates.append(gt)

    gsum = gates[0]
    for gt in gates[1:]:
        gsum = gsum + gt

    sel_all = jnp.concatenate(sel, axis=0)
    before = _dot(sel_all.astype(BF16), u_ref[...]) + run_ref[...]
    run_new = run_ref[...] + jnp.sum(sel_all, axis=1, keepdims=True)
    run_ref[...] = run_new
    cnt_ref[...] = jnp.broadcast_to(run_new, cnt_ref.shape).astype(I32)

    te_ref[...] = jnp.zeros_like(te_ref)
    gate_ref[...] = jnp.zeros_like(gate_ref)
    rank_ref[...] = jnp.zeros_like(rank_ref)
    for j in range(TOP_K):
        rk = jnp.zeros((1, tm), F32)
        for g in range(N_GROUPS):
            rk = rk + jnp.sum(jnp.where(eidx[g] == top_e[j], before[g * per_group:(g + 1) * per_group], 0.0),
                              axis=0, keepdims=True)
        te_ref[j:j + 1, :] = top_e[j]
        gate_ref[j:j + 1, :] = gates[j] / gsum * ROUTED_SCALE
        rank_ref[j:j + 1, :] = rk.astype(I32)


def _route(x2, g, sc, sh, w_router, bias, layer, seq):
    t, d = x2.shape
    tm = ROUTE_TM
    e = N_EXPERTS
    jj = lax.broadcasted_iota(I32, (tm, tm), 0)
    ss = lax.broadcasted_iota(I32, (tm, tm), 1)
    u = (jj < ss).astype(BF16)
    per_seq = seq // tm
    kern = functools.partial(_router_kernel, tm=tm)
    rows = SUBLANES
    depth = w_router.shape[0]
    h, te, gate, rank, cnt = pl.pallas_call(
        kern,
        grid=(t // tm,),
        in_specs=[
            pl.BlockSpec((tm, d), lambda i: (i, 0)),
            pl.BlockSpec((1, d), lambda i: (0, 0)),
            pl.BlockSpec((1, 1, d), lambda i: (i // per_seq, 0, 0)),
            pl.BlockSpec((1, 1, d), lambda i: (i // per_seq, 0, 0)),
            pl.BlockSpec((1, e, d), lambda i: (layer, 0, 0)),
            pl.BlockSpec((1, e, 1), lambda i: (layer, 0, 0)),
            pl.BlockSpec((tm, tm), lambda i: (0, 0)),
        ],
        out_specs=[
            pl.BlockSpec((tm, d), lambda i: (i, 0)),
            pl.BlockSpec((rows, tm), lambda i: (0, i)),
            pl.BlockSpec((rows, tm), lambda i: (0, i)),
            pl.BlockSpec((rows, tm), lambda i: (0, i)),
            pl.BlockSpec((e, LANES), lambda i: (0, 0)),
        ],
        out_shape=[
            jax.ShapeDtypeStruct((t, d), F32),
            jax.ShapeDtypeStruct((rows, t), I32),
            jax.ShapeDtypeStruct((rows, t), F32),
            jax.ShapeDtypeStruct((rows, t), I32),
            jax.ShapeDtypeStruct((e, LANES), I32),
        ],
        scratch_shapes=[pltpu.VMEM((e, 1), F32)],
        compiler_params=_cparams(("arbitrary",)),
        name="norm_router_topk",
    )(x2, g.reshape(1, d), sc, sh, w_router, bias.reshape(depth, e, 1), u)
    return h, te[:TOP_K], gate[:TOP_K], rank[:TOP_K], cnt[:, 0]


def _dispatch_kernel(slot_ref, pe_ref, nu_ref, h_ref, xs_hbm, zbuf, sem, zsem, *, tm, blk, n_blocks):
    i = pl.program_id(0)
    n_used = nu_ref[0]

    def zero_copy(block):
        return pltpu.make_async_copy(zbuf, xs_hbm.at[pl.ds(block * blk, blk)], zsem)

    def expert_has_rows(e):
        return pe_ref[e] > (pe_ref[e - 1] if e else 0)

    @pl.when(i == 0)
    def _():
        zbuf[...] = jnp.zeros_like(zbuf)
        for phase in ("start", "wait"):
            for e in range(N_EXPERTS):
                @pl.when(expert_has_rows(e))
                def _():
                    cp = zero_copy(pe_ref[e] // blk - 1)
                    cp.start() if phase == "start" else cp.wait()

            def tail(bk, carry):
                cp = zero_copy(bk)
                cp.start() if phase == "start" else cp.wait()
                return carry

            lax.fori_loop(n_used, n_blocks, tail, 0)

    base = i * (TOP_K * tm)
    for j in range(TOP_K):
        def body(r, carry):
            dst = slot_ref[base + j * tm + r]
            pltpu.make_async_copy(h_ref.at[pl.ds(r, 1)], xs_hbm.at[pl.ds(dst, 1)], sem).start()
            return carry

        lax.fori_loop(0, tm, body, 0, unroll=8)
    for j in range(TOP_K):
        pltpu.make_async_copy(h_ref, xs_hbm.at[pl.ds(0, tm)], sem).wait()


def _dispatch(h, slots_tiled, pad_end, n_used, n_blocks):
    t, d = h.shape
    tm = DISP_TM
    blk = MOE_BLK
    kern = functools.partial(_dispatch_kernel, tm=tm, blk=blk, n_blocks=n_blocks)
    grid_spec = pltpu.PrefetchScalarGridSpec(
        num_scalar_prefetch=3,
        grid=(t // tm,),
        in_specs=[pl.BlockSpec((tm, d), lambda i, sl, pe, nu: (i, 0))],
        out_specs=pl.BlockSpec(memory_space=pl.ANY),
        scratch_shapes=[
            pltpu.VMEM((blk, d), F32),
            pltpu.SemaphoreType.DMA,
            pltpu.SemaphoreType.DMA,
        ],
    )
    return pl.pallas_call(
        kern,
        grid_spec=grid_spec,
        out_shape=jax.ShapeDtypeStruct((n_blocks * blk, d), F32),
        compiler_params=_cparams(("arbitrary",)),
        name="dispatch_rows",
    )(slots_tiled, pad_end, n_used, h)


def _expert_kernel(be_ref, nu_ref, x_ref, wg_ref, wu_ref, wd_ref, y_ref, wgb, wub, wdb):
    i = pl.program_id(0)
    n_used = nu_ref[0]

    @pl.when(i < n_used)
    def _():
        prev = be_ref[jnp.maximum(i - 1, 0)]

        @pl.when((i == 0) | (be_ref[i] != prev))
        def _():
            wgb[...] = wg_ref[0, 0].astype(BF16)
            wub[...] = wu_ref[0, 0].astype(BF16)
            wdb[...] = wd_ref[0, 0].astype(BF16)

        x = x_ref[...].astype(BF16)
        mid = _silu(_dot(x, wgb[...])) * _dot(x, wub[...])
        y_ref[...] = _dot(mid.astype(BF16), wdb[...])

    @pl.when(i >= n_used)
    def _():
        y_ref[...] = jnp.zeros_like(y_ref)


def _experts(xs, w_gate, w_up, w_down, layer, block_e, n_used):
    n_slots, d = xs.shape
    f = w_gate.shape[-1]
    blk = MOE_BLK
    n_blocks = n_slots // blk

    def w_map(i, be, nu):
        return (layer, be[i], 0, 0)

    grid_spec = pltpu.PrefetchScalarGridSpec(
        num_scalar_prefetch=2,
        grid=(n_blocks,),
        in_specs=[
            pl.BlockSpec((blk, d), lambda i, be, nu: (jnp.minimum(i, nu[0] - 1), 0)),
            pl.BlockSpec((1, 1, d, f), w_map),
            pl.BlockSpec((1, 1, d, f), w_map),
            pl.BlockSpec((1, 1, f, d), w_map),
        ],
        out_specs=pl.BlockSpec((blk, d), lambda i, be, nu: (i, 0)),
        scratch_shapes=[
            pltpu.VMEM((d, f), BF16),
            pltpu.VMEM((d, f), BF16),
            pltpu.VMEM((f, d), BF16),
        ],
    )
    return pl.pallas_call(
        _expert_kernel,
        grid_spec=grid_spec,
        out_shape=jax.ShapeDtypeStruct((n_slots, d), F32),
        compiler_params=_cparams(("arbitrary",)),
        name="routed_experts",
    )(block_e, n_used, xs, w_gate, w_up, w_down)


def _combine_kernel(slot_ref, y_hbm, h_ref, x_ref, g_ref, gw_ref, wg_ref, wu_ref, wd_ref, o_ref,
                    ybuf, sem, *, tm):
    i = pl.program_id(0)
    n = pl.num_programs(0)
    rows = TOP_K * tm

    def issue(tile, slot):
        base = tile * rows

        def body(r, carry):
            src = slot_ref[base + r]
            pltpu.make_async_copy(y_hbm.at[pl.ds(src, 1)], ybuf.at[slot, pl.ds(r, 1)], sem.at[slot]).start()
            return carry

        lax.fori_loop(0, rows, body, 0, unroll=8)

    @pl.when(i == 0)
    def _():
        issue(0, 0)

    @pl.when(i + 1 < n)
    def _():
        issue(i + 1, (i + 1) % 2)

    hb = h_ref[...].astype(BF16)
    mid = _silu(_dot(hb, wg_ref[0])) * _dot(hb, wu_ref[0])
    total = _dot(mid.astype(BF16), wd_ref[0])

    slot = i % 2
    pltpu.make_async_copy(y_hbm.at[pl.ds(0, rows)], ybuf.at[slot], sem.at[slot]).wait()
    gw = gw_ref[...]
    for j in range(TOP_K):
        total = total + gw[:, j:j + 1] * ybuf[slot, j * tm:(j + 1) * tm, :]
    o_ref[...] = x_ref[...] + g_ref[0] * total


def _combine(slots_tiled, y_slots, h, x2, gate, gate_w, ws_gate, ws_up, ws_down, layer, seq):
    t, d = h.shape
    f = ws_gate.shape[-1]
    tm = COMB_TM
    per_seq = seq // tm
    kern = functools.partial(_combine_kernel, tm=tm)
    grid_spec = pltpu.PrefetchScalarGridSpec(
        num_scalar_prefetch=1,
        grid=(t // tm,),
        in_specs=[
            pl.BlockSpec(memory_space=pl.ANY),
            pl.BlockSpec((tm, d), lambda i, sl: (i, 0)),
            pl.BlockSpec((tm, d), lambda i, sl: (i, 0)),
            pl.BlockSpec((1, 1, d), lambda i, sl: (i // per_seq, 0, 0)),
            pl.BlockSpec((tm, SUBLANES), lambda i, sl: (i, 0)),
            pl.BlockSpec((1, d, f), lambda i, sl: (layer, 0, 0)),
            pl.BlockSpec((1, d, f), lambda i, sl: (layer, 0, 0)),
            pl.BlockSpec((1, f, d), lambda i, sl: (layer, 0, 0)),
        ],
        out_specs=pl.BlockSpec((tm, d), lambda i, sl: (i, 0)),
        scratch_shapes=[
            pltpu.VMEM((2, TOP_K * tm, d), F32),
            pltpu.SemaphoreType.DMA((2,)),
        ],
    )
    return pl.pallas_call(
        kern,
        grid_spec=grid_spec,
        out_shape=jax.ShapeDtypeStruct((t, d), F32),
        compiler_params=_cparams(("arbitrary",)),
        name="shared_expert_combine",
    )(slots_tiled, y_slots, h, x2, gate, gate_w, ws_gate, ws_up, ws_down)


def _tile_slots(slot, tm):
    k, t = slot.shape
    return slot.reshape(k, t // tm, tm).transpose(1, 0, 2).reshape(-1)


def _moe_layer(x, g, sc, sh, gate, w_router, router_bias, w_gate, w_up, w_down,
               ws_gate_b, ws_up_b, ws_down_b, layer):
    b, s, d = x.shape
    t = b * s
    x2 = x.reshape(t, d)
    h, top_e, gates, rank, counts = _route(x2, g, sc, sh, w_router, router_bias, layer, s)

    blk = MOE_BLK
    n_blocks = -(-(t * TOP_K) // blk) + N_EXPERTS
    padded = (counts + blk - 1) // blk * blk
    pad_end = jnp.cumsum(padded).astype(I32)
    pad_start = pad_end - padded
    experts = jnp.arange(N_EXPERTS, dtype=I32)
    slot = jnp.sum(jnp.where(top_e[..., None] == experts, pad_start, 0), axis=-1) + rank
    block_first = jnp.arange(n_blocks, dtype=I32) * blk
    block_e = jnp.minimum(jnp.sum((pad_end[None, :] <= block_first[:, None]).astype(I32), axis=1), N_EXPERTS - 1)
    n_used = pad_end[-1:] // blk

    xs = _dispatch(h, _tile_slots(slot, DISP_TM), pad_end, n_used, n_blocks)
    y_slots = _experts(xs, w_gate, w_up, w_down, layer, block_e, n_used)
    gate_w = jnp.pad(gates.T, ((0, 0), (0, SUBLANES - TOP_K)))
    out = _combine(_tile_slots(slot, COMB_TM), y_slots, h, x2, gate, gate_w,
                   ws_gate_b, ws_up_b, ws_down_b, layer, s)
    return out.reshape(b, s, d)


def kernel(x, c, w_mod, b_mod, norm_mix, norm_ffn, w_in, w_out, q_norm, k_norm, lambda_q1, lambda_k1,
           lambda_q2, lambda_k2, subln, w_router, router_bias, w_gate, w_up, w_down, ws_gate, ws_up, ws_down):
    b, s, d = x.shape
    depth = w_mod.shape[0]
    assert d == SB_HEADS * HEAD_DIM == 2 * DA_HEADS * HEAD_DIM
    assert s % max(ATT_TQ, PROJ_TM, ROUTE_TM, DISP_TM) == 0
    q_scale = HEAD_DIM ** -0.5 * LOG2E

    mod = _modulation(c, w_mod, b_mod)
    ws_gate_b, ws_up_b, ws_down_b = ws_gate.astype(BF16), ws_up.astype(BF16), ws_down.astype(BF16)
    for i in range(depth):
        sh1, sc1, g1, sh2, sc2, g2 = [m.reshape(b, 1, d) for m in jnp.split(mod[i], 6, axis=-1)]
        w_in_b = w_in[i].astype(BF16)
        if i % 2 == 0:
            colscale = jnp.stack([jnp.full((1, d), q_scale, F32), jnp.ones((1, d), F32), jnp.ones((1, d), F32)])
            qk, v = _qkv_proj(x, norm_mix[i], sc1, sh1, w_in_b, colscale, qk_norm=False, v_width=HEAD_DIM)
            o = _sb_attention(qk, v, d)
        else:
            j = i // 2
            lambda_init = 0.8 - 0.6 * math.exp(-0.3 * i)
            reps = d // HEAD_DIM
            colscale = jnp.stack([
                jnp.tile(q_norm[j].astype(F32), reps)[None, :] * q_scale,
                jnp.tile(k_norm[j].astype(F32), reps)[None, :],
                jnp.ones((1, d), F32)])
            qk, v = _qkv_proj(x, norm_mix[i], sc1, sh1, w_in_b, colscale, qk_norm=True, v_width=2 * HEAD_DIM)
            lam_rows = jnp.stack([lambda_q1[j], lambda_k1[j], lambda_q2[j], lambda_k2[j]]).astype(F32)
            o = _da_attention(qk, v, lam_rows, subln[j].astype(F32), d, lambda_init)
        x = _out_proj(o, w_out[i].astype(BF16), x, g1)
        x = _moe_layer(x, norm_ffn[i], sc2, sh2, g2, w_router, router_bias, w_gate, w_up, w_down,
                       ws_gate_b, ws_up_b, ws_down_b, i)
    return x
```

```python
import functools
import math

import jax
import jax.numpy as jnp
from jax import lax
from jax.experimental import pallas as pl
from jax.experimental.pallas import tpu as pltpu

F32 = jnp.float32
BF16 = jnp.bfloat16
I32 = jnp.int32

LANES = 128
SUBLANES = 8
VMEM_LIMIT = 56 * 1024 * 1024

SB_HEADS = 16
DA_HEADS = 8
HEAD_DIM = 128
N_EXPERTS = 64
TOP_K = 6
N_GROUPS = 8
TOPK_GROUPS = 4
ROUTED_SCALE = 2.5
EPS = 1e-6
SUBLN_EPS = 1e-5
LOG2E = math.log2(math.e)

SB_TQ, SB_TK = 512, 256
DA_TQ, DA_TK = 256, 512
PROJ_TM = 512
ROUTE_TM = 512
MOE_BLK = 256
DISP_TM = 512
COMB_TM = 128


def _cparams(sem):
    return pltpu.CompilerParams(dimension_semantics=sem, vmem_limit_bytes=VMEM_LIMIT)


def _silu(x):
    return x * jax.nn.sigmoid(x)


def _dot(a, b):
    return jnp.dot(a, b, preferred_element_type=F32)


def _dot_nt(a, b):
    return lax.dot_general(a, b, (((1,), (1,)), ((), ())), preferred_element_type=F32)


def _store_slabs(ref, base, rows, vals):
    slab = vals.shape[1] // LANES
    for s in range(slab):
        ref[pl.ds(base + s, rows, stride=slab), :] = vals[:, s * LANES:(s + 1) * LANES]


def _load_slabs(ref, base, rows, slab, pitch=None):
    return [ref[pl.ds(base + s, rows, stride=pitch or slab), :] for s in range(slab)]


def _mod_kernel(c_ref, w_ref, b_ref, o_ref):
    cond = _silu(c_ref[...])
    o_ref[0] = _dot(cond.astype(BF16), w_ref[0].astype(BF16)) + b_ref[0]


def _modulation(c, w_mod, b_mod):
    depth, d, n = w_mod.shape
    b = c.shape[0]
    rows = -(-b // SUBLANES) * SUBLANES
    cp = jnp.pad(c, ((0, rows - b), (0, 0)))
    tn = 1024
    out = pl.pallas_call(
        _mod_kernel,
        grid=(depth, n // tn),
        in_specs=[
            pl.BlockSpec((rows, d), lambda l, j: (0, 0)),
            pl.BlockSpec((1, d, tn), lambda l, j: (l, 0, j)),
            pl.BlockSpec((1, 1, tn), lambda l, j: (l, 0, j)),
        ],
        out_specs=pl.BlockSpec((1, rows, tn), lambda l, j: (l, 0, j)),
        out_shape=jax.ShapeDtypeStruct((depth, rows, n), F32),
        compiler_params=_cparams(("arbitrary", "arbitrary")),
        name="adaln_mod",
    )(cp, w_mod, b_mod.reshape(depth, 1, n))
    return out[:, :b]


def _norm_mod(x, g, sc, sh):
    ms = jnp.mean(x * x, axis=-1, keepdims=True)
    return x * lax.rsqrt(ms + EPS) * g * (1.0 + sc) + sh


def _qkv_kernel(x_ref, g_ref, sc_ref, sh_ref, w_ref, cs_ref, qk_ref, v_ref, h_ref, *, qk_norm, v_width):
    j = pl.program_id(2)

    @pl.when(j == 0)
    def _():
        h_ref[...] = _norm_mod(x_ref[0], g_ref[...], sc_ref[0], sh_ref[0]).astype(BF16)

    acc = _dot(h_ref[...], w_ref[...])
    n_chunks = acc.shape[1] // HEAD_DIM

    @pl.when(j < 2)
    def _():
        for c in range(n_chunks):
            a = acc[:, c * HEAD_DIM:(c + 1) * HEAD_DIM]
            if qk_norm:
                a = a * lax.rsqrt(jnp.mean(a * a, axis=-1, keepdims=True) + EPS)
            a = a * cs_ref[0][:, c * HEAD_DIM:(c + 1) * HEAD_DIM]
            qk_ref[0, c] = a.astype(BF16)

    @pl.when(j == 2)
    def _():
        for c in range(acc.shape[1] // v_width):
            v_ref[0, c] = acc[:, c * v_width:(c + 1) * v_width].astype(BF16)


def _qkv_proj(x, g, sc, sh, w_bf16, colscale, *, qk_norm, v_width):
    b, s, d = x.shape
    tm = PROJ_TM
    n_ch = d // HEAD_DIM
    kern = functools.partial(_qkv_kernel, qk_norm=qk_norm, v_width=v_width)
    return pl.pallas_call(
        kern,
        grid=(b, s // tm, 3),
        in_specs=[
            pl.BlockSpec((1, tm, d), lambda bi, i, j: (bi, i, 0)),
            pl.BlockSpec((1, d), lambda bi, i, j: (0, 0)),
            pl.BlockSpec((1, 1, d), lambda bi, i, j: (bi, 0, 0)),
            pl.BlockSpec((1, 1, d), lambda bi, i, j: (bi, 0, 0)),
            pl.BlockSpec((d, d), lambda bi, i, j: (0, j)),
            pl.BlockSpec((1, 1, d), lambda bi, i, j: (j, 0, 0)),
        ],
        out_specs=[
            pl.BlockSpec((1, n_ch, tm, HEAD_DIM), lambda bi, i, j: (bi, jnp.minimum(j, 1), i, 0)),
            pl.BlockSpec((1, d // v_width, tm, v_width), lambda bi, i, j: (bi, 0, i, 0)),
        ],
        out_shape=[
            jax.ShapeDtypeStruct((b, 2 * n_ch, s, HEAD_DIM), BF16),
            jax.ShapeDtypeStruct((b, d // v_width, s, v_width), BF16),
        ],
        scratch_shapes=[pltpu.VMEM((tm, d), BF16)],
        compiler_params=_cparams(("arbitrary", "arbitrary", "arbitrary")),
        name="norm_qkv_proj",
    )(x, g.reshape(1, d), sc, sh, w_bf16, colscale)


def _sb_kernel(q_ref, k_ref, v_ref, u_ref, o_ref, *, tq, tk):
    i = pl.program_id(2)
    q = q_ref[0, 0]
    per = tq // tk
    rows = i * tq + lax.broadcasted_iota(I32, (tq, 1), 0)
    u = u_ref[...]

    def chunks_of(g):
        return [per * (i - g) + (per - 1 - r) for r in range(per)]

    def logits(g):
        out = []
        for c in chunks_of(g):
            k = k_ref[0, 0, pl.ds(pl.multiple_of(c * tk, tk), tk), :]
            out.append(_dot_nt(q, k))
        return tuple(out)

    def weights(g, zs, run, masked):
        out = []
        for c, z in zip(chunks_of(g), zs):
            softplus = jnp.maximum(z, 0.0) + jnp.log(1.0 + jnp.exp2(-jnp.abs(z))) * LOG2E
            if masked:
                earlier = (c * tk + lax.broadcasted_iota(I32, (1, tk), 1)) < rows
                softplus = jnp.where(earlier, softplus, 0.0)
            between = _dot(softplus.astype(BF16), u) + run
            a = jnp.exp2((z - softplus) + between)
            if masked:
                a = jnp.where(earlier, a, 0.0)
            out.append(a.astype(BF16))
            run = run - jnp.sum(softplus, axis=-1, keepdims=True)
        return tuple(out), run

    def values(g, ws, acc):
        for c, a in zip(chunks_of(g), ws):
            v = v_ref[0, 0, pl.ds(pl.multiple_of(c * tk, tk), tk), :]
            acc = acc + _dot(a, v)
        return acc

    def group(g, run, acc, masked):
        ws, run = weights(g, logits(g), run, masked)
        return run, values(g, ws, acc)

    run, acc = group(0, jnp.zeros((tq, 1), F32), jnp.zeros((tq, HEAD_DIM), F32), True)
    _, acc = lax.fori_loop(1, i + 1, lambda g, c: group(g, c[0], c[1], False), (run, acc))
    o_ref[0] = acc.astype(BF16)


def _sb_attention(qk, v, d):
    b, _, s, _ = qk.shape
    tq, tk = SB_TQ, SB_TK
    heads = SB_HEADS
    jj = lax.broadcasted_iota(I32, (tk, tk), 0)
    ss = lax.broadcasted_iota(I32, (tk, tk), 1)
    u = jnp.where(jj > ss, -1.0, 0.0).astype(BF16)
    kern = functools.partial(_sb_kernel, tq=tq, tk=tk)
    return pl.pallas_call(
        kern,
        grid=(b, heads, s // tq),
        in_specs=[
            pl.BlockSpec((1, 1, tq, HEAD_DIM), lambda bi, h, i: (bi, h, i, 0)),
            pl.BlockSpec((1, 1, s, HEAD_DIM), lambda bi, h, i: (bi, heads + h, 0, 0)),
            pl.BlockSpec((1, 1, s, HEAD_DIM), lambda bi, h, i: (bi, h, 0, 0)),
            pl.BlockSpec((tk, tk), lambda bi, h, i: (0, 0)),
        ],
        out_specs=pl.BlockSpec((1, tq, HEAD_DIM), lambda bi, h, i: (bi, i, h)),
        out_shape=jax.ShapeDtypeStruct((b, s, d), BF16),
        compiler_params=_cparams(("arbitrary", "arbitrary", "arbitrary")),
        name="stickbreak_attn",
    )(qk, qk, v, u)


def _da_kernel(q_ref, k_ref, v_ref, slope_ref, lam_ref, subln_ref, o_ref, *, tq, tk, lambda_init):
    i = pl.program_id(2)
    per = tq // tk
    row0 = i * tq
    rows = row0 + lax.broadcasted_iota(I32, (tq, 1), 0)
    slope = slope_ref[0]
    qs = (q_ref[0, 0], q_ref[0, 1])
    dv = v_ref.shape[-1]

    def chunk(c, carry, masked):
        start = pl.multiple_of(c * tk, tk)
        cols = start + lax.broadcasted_iota(I32, (1, tk), 1)
        key_bias = slope * (cols - row0).astype(F32)
        v = v_ref[0, 0, pl.ds(start, tk), :]
        new = []
        for m in range(2):
            mx, den, acc = carry[3 * m:3 * m + 3]
            k = k_ref[0, m, pl.ds(start, tk), :]
            z = _dot_nt(qs[m], k) + key_bias
            if masked:
                z = jnp.where(cols <= rows, z, -jnp.inf)
            mx_new = jnp.maximum(mx, jnp.max(z, axis=-1, keepdims=True))
            alpha = jnp.exp2(mx - mx_new)
            p = jnp.exp2(z - mx_new)
            den = alpha * den + jnp.sum(p, axis=-1, keepdims=True)
            acc = alpha * acc + _dot(p.astype(BF16), v)
            new += [mx_new, den, acc]
        return tuple(new)

    init = []
    for _ in range(2):
        init += [jnp.full((tq, 1), -jnp.inf, F32), jnp.zeros((tq, 1), F32), jnp.zeros((tq, dv), F32)]
    n_below = (i * tq) // tk
    carry = lax.fori_loop(0, n_below, lambda c, cr: chunk(c, cr, False), tuple(init))
    for r in range(max(1, per)):
        carry = chunk(n_below + r, carry, True)
    _, l1, a1, _, l2, a2 = carry

    lv = lam_ref[...]
    s1 = jnp.sum(lv[0:1] * lv[1:2], axis=-1, keepdims=True)
    s2 = jnp.sum(lv[2:3] * lv[3:4], axis=-1, keepdims=True)
    lam = jnp.exp(s1) - jnp.exp(s2) + lambda_init
    o = a1 / l1 - lam * (a2 / l2)
    o = o * lax.rsqrt(jnp.mean(o * o, axis=-1, keepdims=True) + SUBLN_EPS) * subln_ref[...]
    o_ref[0] = (o * (1.0 - lambda_init)).astype(BF16)


def _da_attention(qk, v, lam_rows, subln, d, lambda_init):
    b, _, s, _ = qk.shape
    tq, tk = DA_TQ, DA_TK
    heads = DA_HEADS
    dv = v.shape[-1]
    slopes = jnp.asarray(
        [2.0 ** (-8.0 * (h + 1) / heads) * LOG2E for h in range(heads)], F32).reshape(heads, 1, 1)
    kern = functools.partial(_da_kernel, tq=tq, tk=tk, lambda_init=lambda_init)
    return pl.pallas_call(
        kern,
        grid=(b, heads, s // tq),
        in_specs=[
            pl.BlockSpec((1, 2, tq, HEAD_DIM), lambda bi, h, i: (bi, h, i, 0)),
            pl.BlockSpec((1, 2, s, HEAD_DIM), lambda bi, h, i: (bi, heads + h, 0, 0)),
            pl.BlockSpec((1, 1, s, dv), lambda bi, h, i: (bi, h, 0, 0)),
            pl.BlockSpec((1, 1, 1), lambda bi, h, i: (h, 0, 0)),
            pl.BlockSpec((4, HEAD_DIM), lambda bi, h, i: (0, 0)),
            pl.BlockSpec((1, dv), lambda bi, h, i: (0, 0)),
        ],
        out_specs=pl.BlockSpec((1, tq, dv), lambda bi, h, i: (bi, i, h)),
        out_shape=jax.ShapeDtypeStruct((b, s, d), BF16),
        compiler_params=_cparams(("arbitrary", "arbitrary", "arbitrary")),
        name="diff_attn",
    )(qk, qk, v, slopes, lam_rows, subln.reshape(1, dv))


def _outproj_kernel(o_ref, w_ref, x_ref, g_ref, y_ref):
    y_ref[0] = x_ref[0] + g_ref[0] * _dot(o_ref[0], w_ref[...])


def _out_proj(o, w_bf16, x, gate):
    b, s, d = x.shape
    tm = PROJ_TM
    return pl.pallas_call(
        _outproj_kernel,
        grid=(b, s // tm),
        in_specs=[
            pl.BlockSpec((1, tm, d), lambda bi, i: (bi, i, 0)),
            pl.BlockSpec((d, d), lambda bi, i: (0, 0)),
            pl.BlockSpec((1, tm, d), lambda bi, i: (bi, i, 0)),
            pl.BlockSpec((1, 1, d), lambda bi, i: (bi, 0, 0)),
        ],
        out_specs=pl.BlockSpec((1, tm, d), lambda bi, i: (bi, i, 0)),
        out_shape=jax.ShapeDtypeStruct((b, s, d), F32),
        compiler_params=_cparams(("arbitrary", "arbitrary")),
        name="out_proj_residual",
    )(o, w_bf16, x, gate)


def _router_kernel(x_ref, g_ref, sc_ref, sh_ref, wr_ref, bias_ref, u_ref,
                   h_ref, slab_ref, te_ref, gate_ref, rank_ref, cnt_ref, run_ref, *, tm):
    i = pl.program_id(0)
    per_group = N_EXPERTS // N_GROUPS

    @pl.when(i == 0)
    def _():
        run_ref[...] = jnp.zeros_like(run_ref)

    h = _norm_mod(x_ref[...], g_ref[...], sc_ref[0], sh_ref[0])
    h_ref[...] = h.astype(BF16)
    _store_slabs(slab_ref, 0, tm, h)
    logits = lax.dot_general(wr_ref[0], h, (((1,), (1,)), ((), ())),
                             precision=lax.Precision.HIGHEST, preferred_element_type=F32)
    scores = jax.nn.sigmoid(logits)
    biased = scores + bias_ref[0]
    neg = -jnp.inf
    sub = lax.broadcasted_iota(I32, (per_group, tm), 0)

    def first_max(vals, idx, sentinel):
        m = jnp.max(vals, axis=0, keepdims=True)
        return m, jnp.min(jnp.where(vals == m, idx, sentinel), axis=0, keepdims=True)

    sc_g = [scores[g * per_group:(g + 1) * per_group] for g in range(N_GROUPS)]
    bi_g = [biased[g * per_group:(g + 1) * per_group] for g in range(N_GROUPS)]

    gscore = jnp.zeros((N_GROUPS, tm), F32)
    for g in range(N_GROUPS):
        m1, i1 = first_max(bi_g[g], sub, per_group)
        m2 = jnp.max(jnp.where(sub == i1, neg, bi_g[g]), axis=0, keepdims=True)
        gscore = jnp.where(sub == g, m1 + m2, gscore)

    keep = jnp.zeros((N_GROUPS, tm), F32)
    cur = gscore
    for _ in range(TOPK_GROUPS):
        _, gi = first_max(cur, sub, N_GROUPS)
        hit = sub == gi
        keep = jnp.where(hit, 1.0, keep)
        cur = jnp.where(hit, neg, cur)

    masked = []
    for g in range(N_GROUPS):
        kg = jnp.max(jnp.where(sub == g, keep, 0.0), axis=0, keepdims=True)
        masked.append(jnp.where(kg > 0.0, bi_g[g], neg))
    eidx = [sub + g * per_group for g in range(N_GROUPS)]
    sel = [jnp.zeros((per_group, tm), F32) for _ in range(N_GROUPS)]

    top_e, gates = [], []
    for _ in range(TOP_K):
        m = masked[0].max(axis=0, keepdims=True)
        for g in range(1, N_GROUPS):
            m = jnp.maximum(m, jnp.max(masked[g], axis=0, keepdims=True))
        ei = jnp.full((1, tm), N_EXPERTS, I32)
        for g in range(N_GROUPS):
            ei = jnp.minimum(ei, jnp.min(jnp.where(masked[g] == m, eidx[g], N_EXPERTS), axis=0, keepdims=True))
        gt = jnp.zeros((1, tm), F32)
        for g in range(N_GROUPS):
            hit = eidx[g] == ei
            gt = gt + jnp.sum(jnp.where(hit, sc_g[g], 0.0), axis=0, keepdims=True)
            masked[g] = jnp.where(hit, neg, masked[g])
            sel[g] = jnp.where(hit, 1.0, sel[g])
        top_e.append(ei)
        gates.append(gt)

    gsum = gates[0]
    for gt in gates[1:]:
        gsum = gsum + gt

    sel_all = jnp.concatenate(sel, axis=0)
    before = _dot(sel_all.astype(BF16), u_ref[...]) + run_ref[...]
    run_new = run_ref[...] + jnp.sum(sel_all, axis=1, keepdims=True)
    run_ref[...] = run_new
    cnt_ref[...] = jnp.broadcast_to(run_new, cnt_ref.shape).astype(I32)

    te_ref[...] = jnp.zeros_like(te_ref)
    gate_ref[...] = jnp.zeros_like(gate_ref)
    rank_ref[...] = jnp.zeros_like(rank_ref)
    for j in range(TOP_K):
        rk = jnp.zeros((1, tm), F32)
        for g in range(N_GROUPS):
            rk = rk + jnp.sum(jnp.where(eidx[g] == top_e[j], before[g * per_group:(g + 1) * per_group], 0.0),
                              axis=0, keepdims=True)
        te_ref[j:j + 1, :] = top_e[j]
        gate_ref[j:j + 1, :] = gates[j] / gsum * ROUTED_SCALE
        rank_ref[j:j + 1, :] = rk.astype(I32)


def _route(x2, g, sc, sh, w_router, bias, layer, seq):
    t, d = x2.shape
    tm = ROUTE_TM
    e = N_EXPERTS
    slab = d // LANES
    jj = lax.broadcasted_iota(I32, (tm, tm), 0)
    ss = lax.broadcasted_iota(I32, (tm, tm), 1)
    u = (jj < ss).astype(BF16)
    per_seq = seq // tm
    kern = functools.partial(_router_kernel, tm=tm)
    rows = SUBLANES
    depth = w_router.shape[0]
    h, h_slabs, te, gate, rank, cnt = pl.pallas_call(
        kern,
        grid=(t // tm,),
        in_specs=[
            pl.BlockSpec((tm, d), lambda i: (i, 0)),
            pl.BlockSpec((1, d), lambda i: (0, 0)),
            pl.BlockSpec((1, 1, d), lambda i: (i // per_seq, 0, 0)),
            pl.BlockSpec((1, 1, d), lambda i: (i // per_seq, 0, 0)),
            pl.BlockSpec((1, e, d), lambda i: (layer, 0, 0)),
            pl.BlockSpec((1, e, 1), lambda i: (layer, 0, 0)),
            pl.BlockSpec((tm, tm), lambda i: (0, 0)),
        ],
        out_specs=[
            pl.BlockSpec((tm, d), lambda i: (i, 0)),
            pl.BlockSpec((tm * slab, LANES), lambda i: (i, 0)),
            pl.BlockSpec((rows, tm), lambda i: (0, i)),
            pl.BlockSpec((rows, tm), lambda i: (0, i)),
            pl.BlockSpec((rows, tm), lambda i: (0, i)),
            pl.BlockSpec((e, LANES), lambda i: (0, 0)),
        ],
        out_shape=[
            jax.ShapeDtypeStruct((t, d), BF16),
            jax.ShapeDtypeStruct((t * slab, LANES), F32),
            jax.ShapeDtypeStruct((rows, t), I32),
            jax.ShapeDtypeStruct((rows, t), F32),
            jax.ShapeDtypeStruct((rows, t), I32),
            jax.ShapeDtypeStruct((e, LANES), I32),
        ],
        scratch_shapes=[pltpu.VMEM((e, 1), F32)],
        compiler_params=_cparams(("arbitrary",)),
        name="norm_router_topk",
    )(x2, g.reshape(1, d), sc, sh, w_router, bias.reshape(depth, e, 1), u)
    return h, h_slabs, te[:TOP_K], gate[:TOP_K], rank[:TOP_K], cnt[:, 0]


def _dispatch_kernel(slot_ref, pe_ref, nu_ref, h_ref, xs_hbm, zbuf, sem, zsem, *, tm, blk, n_blocks, slab):
    i = pl.program_id(0)
    n_used = nu_ref[0]
    group = 8

    def zero_copy(block):
        start = pl.multiple_of(block * (blk * slab), blk * slab)
        return pltpu.make_async_copy(zbuf, xs_hbm.at[pl.ds(start, blk * slab)], zsem)

    def expert_has_rows(e):
        return pe_ref[e] > (pe_ref[e - 1] if e else 0)

    @pl.when(i == 0)
    def _():
        zbuf[...] = jnp.zeros_like(zbuf)
        for phase in ("start", "wait"):
            for e in range(N_EXPERTS):
                @pl.when(expert_has_rows(e))
                def _():
                    cp = zero_copy(pe_ref[e] // blk - 1)
                    cp.start() if phase == "start" else cp.wait()

            def tail(bk, carry):
                cp = zero_copy(bk)
                cp.start() if phase == "start" else cp.wait()
                return carry

            lax.fori_loop(n_used, n_blocks, tail, 0)

    base = i * (TOP_K * tm)
    for j in range(TOP_K):
        def body(g, carry):
            for r in range(group):
                dst = pl.multiple_of(slot_ref[base + j * tm + g * group + r] * slab, slab)
                src = pl.multiple_of(g * (group * slab), group * slab) + r * slab
                pltpu.make_async_copy(h_ref.at[pl.ds(src, slab)], xs_hbm.at[pl.ds(dst, slab)], sem).start(
                    priority=r % 2)
            return carry

        lax.fori_loop(0, tm // group, body, 0)
    for j in range(TOP_K):
        pltpu.make_async_copy(h_ref, xs_hbm.at[pl.ds(0, tm * slab)], sem).wait()


def _dispatch(h_slabs, slots_tiled, pad_end, n_used, n_blocks, slab):
    t = h_slabs.shape[0] // slab
    tm = DISP_TM
    blk = MOE_BLK
    kern = functools.partial(_dispatch_kernel, tm=tm, blk=blk, n_blocks=n_blocks, slab=slab)
    grid_spec = pltpu.PrefetchScalarGridSpec(
        num_scalar_prefetch=3,
        grid=(t // tm,),
        in_specs=[pl.BlockSpec((tm * slab, LANES), lambda i, sl, pe, nu: (i, 0))],
        out_specs=pl.BlockSpec(memory_space=pl.ANY),
        scratch_shapes=[
            pltpu.VMEM((blk * slab, LANES), F32),
            pltpu.SemaphoreType.DMA,
            pltpu.SemaphoreType.DMA,
        ],
    )
    return pl.pallas_call(
        kern,
        grid_spec=grid_spec,
        out_shape=jax.ShapeDtypeStruct((n_blocks * blk * slab, LANES), F32),
        compiler_params=_cparams(("arbitrary",)),
        name="dispatch_rows",
    )(slots_tiled, pad_end, n_used, h_slabs)


def _expert_kernel(be_ref, eo_ref, ne_ref, nu_ref, x_ref, wg_hbm, wu_hbm, wd_hbm, y_ref,
                   wg32, wu32, wd32, wgb, wub, wdb, sem, *, blk, slab, layer):
    i = pl.program_id(0)
    n_used = nu_ref[0]

    def weight_copies(e, slot):
        return [pltpu.make_async_copy(src.at[layer, e], dst.at[slot], sem.at[slot])
                for src, dst in ((wg_hbm, wg32), (wu_hbm, wu32), (wd_hbm, wd32))]

    @pl.when(i == 0)
    def _():
        for cp in weight_copies(be_ref[0], 0):
            cp.start()

    @pl.when(i < n_used)
    def _():
        prev = be_ref[jnp.maximum(i - 1, 0)]

        @pl.when((i == 0) | (be_ref[i] != prev))
        def _():
            slot = eo_ref[i] % 2
            for cp in weight_copies(be_ref[i], slot):
                cp.wait()
            nxt = ne_ref[i]

            @pl.when(nxt >= 0)
            def _():
                for cp in weight_copies(nxt, 1 - slot):
                    cp.start()

            wgb[...] = wg32[slot].astype(BF16)
            wub[...] = wu32[slot].astype(BF16)
            wdb[...] = wd32[slot].astype(BF16)

        x = jnp.concatenate([c.astype(BF16) for c in _load_slabs(x_ref, 0, blk, slab)], axis=1)
        mid = _silu(_dot(x, wgb[...])) * _dot(x, wub[...])
        _store_slabs(y_ref, 0, blk, _dot(mid.astype(BF16), wdb[...]))

    @pl.when(i >= n_used)
    def _():
        y_ref[...] = jnp.zeros_like(y_ref)


def _experts(xs, w_gate, w_up, w_down, layer, block_e, expert_ord, next_expert, n_used, slab):
    d, f = w_gate.shape[-2:]
    blk = MOE_BLK
    n_blocks = xs.shape[0] // (blk * slab)
    kern = functools.partial(_expert_kernel, blk=blk, slab=slab, layer=layer)
    grid_spec = pltpu.PrefetchScalarGridSpec(
        num_scalar_prefetch=4,
        grid=(n_blocks,),
        in_specs=[
            pl.BlockSpec((blk * slab, LANES), lambda i, be, eo, ne, nu: (jnp.minimum(i, nu[0] - 1), 0)),
            pl.BlockSpec(memory_space=pl.ANY),
            pl.BlockSpec(memory_space=pl.ANY),
            pl.BlockSpec(memory_space=pl.ANY),
        ],
        out_specs=pl.BlockSpec((blk * slab, LANES), lambda i, be, eo, ne, nu: (i, 0)),
        scratch_shapes=[
            pltpu.VMEM((2, d, f), F32),
            pltpu.VMEM((2, d, f), F32),
            pltpu.VMEM((2, f, d), F32),
            pltpu.VMEM((d, f), BF16),
            pltpu.VMEM((d, f), BF16),
            pltpu.VMEM((f, d), BF16),
            pltpu.SemaphoreType.DMA((2,)),
        ],
    )
    return pl.pallas_call(
        kern,
        grid_spec=grid_spec,
        out_shape=jax.ShapeDtypeStruct(xs.shape, F32),
        compiler_params=_cparams(("arbitrary",)),
        name="routed_experts",
    )(block_e, expert_ord, next_expert, n_used, xs, w_gate, w_up, w_down)


def _combine_kernel(slot_ref, y_hbm, h_ref, x_ref, g_ref, gw_ref, wg_ref, wu_ref, wd_ref, o_ref,
                    ybuf, sem, *, tm, slab, pitch):
    i = pl.program_id(0)
    n = pl.num_programs(0)
    rows = TOP_K * tm
    group = 8

    def issue(tile, slot):
        base = tile * rows

        def body(g, carry):
            for r in range(group):
                src = pl.multiple_of(slot_ref[base + g * group + r] * slab, slab)
                dst = pl.multiple_of((slot * rows + g * group) * pitch, group * pitch) + r * pitch
                pltpu.make_async_copy(y_hbm.at[pl.ds(src, slab)], ybuf.at[pl.ds(dst, slab)], sem.at[slot]).start(
                    priority=r % 2)
            return carry

        lax.fori_loop(0, rows // group, body, 0)

    @pl.when(i == 0)
    def _():
        issue(0, 0)

    @pl.when(i + 1 < n)
    def _():
        issue(i + 1, (i + 1) % 2)

    hb = h_ref[...]
    mid = _silu(_dot(hb, wg_ref[0])) * _dot(hb, wu_ref[0])
    shared = _dot(mid.astype(BF16), wd_ref[0])

    slot = i % 2
    buf0 = pl.multiple_of(slot * (rows * pitch), rows * pitch)
    pltpu.make_async_copy(y_hbm.at[pl.ds(0, rows * slab)], ybuf.at[pl.ds(0, rows * slab)], sem.at[slot]).wait()
    gw = gw_ref[...]
    gwb = [jnp.broadcast_to(gw[:, j:j + 1], (tm, LANES)) for j in range(TOP_K)]
    chunks = [jnp.zeros((tm, LANES), F32) for _ in range(slab)]
    for j in range(TOP_K):
        for s, y in enumerate(_load_slabs(ybuf, buf0 + j * (tm * pitch), tm, slab, pitch)):
            chunks[s] = chunks[s] + gwb[j] * y
    routed = jnp.concatenate(chunks, axis=1)
    o_ref[...] = x_ref[...] + g_ref[0] * (shared + routed)


def _combine(slots_tiled, y_slots, h, x2, gate, gate_w, ws_gate, ws_up, ws_down, layer, seq, slab):
    t, d = h.shape
    f = ws_gate.shape[-1]
    tm = COMB_TM
    per_seq = seq // tm
    pitch = slab + SUBLANES
    kern = functools.partial(_combine_kernel, tm=tm, slab=slab, pitch=pitch)
    grid_spec = pltpu.PrefetchScalarGridSpec(
        num_scalar_prefetch=1,
        grid=(t // tm,),
        in_specs=[
            pl.BlockSpec(memory_space=pl.ANY),
            pl.BlockSpec((tm, d), lambda i, sl: (i, 0)),
            pl.BlockSpec((tm, d), lambda i, sl: (i, 0)),
            pl.BlockSpec((1, 1, d), lambda i, sl: (i // per_seq, 0, 0)),
            pl.BlockSpec((tm, SUBLANES), lambda i, sl: (i, 0)),
            pl.BlockSpec((1, d, f), lambda i, sl: (layer, 0, 0)),
            pl.BlockSpec((1, d, f), lambda i, sl: (layer, 0, 0)),
            pl.BlockSpec((1, f, d), lambda i, sl: (layer, 0, 0)),
        ],
        out_specs=pl.BlockSpec((tm, d), lambda i, sl: (i, 0)),
        scratch_shapes=[
            pltpu.VMEM((2 * TOP_K * tm * pitch, LANES), F32),
            pltpu.SemaphoreType.DMA((2,)),
        ],
    )
    return pl.pallas_call(
        kern,
        grid_spec=grid_spec,
        out_shape=jax.ShapeDtypeStruct((t, d), F32),
        compiler_params=_cparams(("arbitrary",)),
        name="shared_expert_combine",
    )(slots_tiled, y_slots, h, x2, gate, gate_w, ws_gate, ws_up, ws_down)


def _tile_slots(slot, tm):
    k, t = slot.shape
    return slot.reshape(k, t // tm, tm).transpose(1, 0, 2).reshape(-1)


def _moe_layer(x, g, sc, sh, gate, w_router, router_bias, w_gate, w_up, w_down,
               ws_gate_b, ws_up_b, ws_down_b, layer):
    b, s, d = x.shape
    t = b * s
    x2 = x.reshape(t, d)
    slab = d // LANES
    h, h_slabs, top_e, gates, rank, counts = _route(x2, g, sc, sh, w_router, router_bias, layer, s)

    blk = MOE_BLK
    n_blocks = -(-(t * TOP_K) // blk) + N_EXPERTS
    padded = (counts + blk - 1) // blk * blk
    pad_end = jnp.cumsum(padded).astype(I32)
    pad_start = pad_end - padded
    experts = jnp.arange(N_EXPERTS, dtype=I32)
    slot = jnp.sum(jnp.where(top_e[..., None] == experts, pad_start, 0), axis=-1) + rank
    block_first = jnp.arange(n_blocks, dtype=I32) * blk
    block_e = jnp.minimum(jnp.sum((pad_end[None, :] <= block_first[:, None]).astype(I32), axis=1), N_EXPERTS - 1)
    n_used = pad_end[-1:] // blk
    has_rows = padded > 0
    ordinal = jnp.cumsum(has_rows.astype(I32)) - 1
    later = jnp.where(has_rows[None, :] & (experts[None, :] > experts[:, None]), experts[None, :], N_EXPERTS)
    following = jnp.min(later, axis=1)
    following = jnp.where(following < N_EXPERTS, following, -1)
    block_hot = block_e[:, None] == experts[None, :]
    expert_ord = jnp.sum(jnp.where(block_hot, ordinal[None, :], 0), axis=1).astype(I32)
    next_expert = jnp.sum(jnp.where(block_hot, following[None, :], 0), axis=1).astype(I32)

    xs = _dispatch(h_slabs, _tile_slots(slot, DISP_TM), pad_end, n_used, n_blocks, slab)
    y_slots = _experts(xs, w_gate, w_up, w_down, layer, block_e, expert_ord, next_expert, n_used, slab)
    gate_w = jnp.pad(gates.T, ((0, 0), (0, SUBLANES - TOP_K)))
    out = _combine(_tile_slots(slot, COMB_TM), y_slots, h, x2, gate, gate_w,
                   ws_gate_b, ws_up_b, ws_down_b, layer, s, slab)
    return out.reshape(b, s, d)


def kernel(x, c, w_mod, b_mod, norm_mix, norm_ffn, w_in, w_out, q_norm, k_norm, lambda_q1, lambda_k1,
           lambda_q2, lambda_k2, subln, w_router, router_bias, w_gate, w_up, w_down, ws_gate, ws_up, ws_down):
    b, s, d = x.shape
    depth = w_mod.shape[0]
    assert d == SB_HEADS * HEAD_DIM == 2 * DA_HEADS * HEAD_DIM
    assert s % max(SB_TQ, SB_TK, DA_TQ, DA_TK, PROJ_TM, ROUTE_TM, DISP_TM) == 0
    q_scale = HEAD_DIM ** -0.5 * LOG2E

    mod = _modulation(c, w_mod, b_mod)
    ws_gate_b, ws_up_b, ws_down_b = ws_gate.astype(BF16), ws_up.astype(BF16), ws_down.astype(BF16)
    for i in range(depth):
        sh1, sc1, g1, sh2, sc2, g2 = [m.reshape(b, 1, d) for m in jnp.split(mod[i], 6, axis=-1)]
        w_in_b = w_in[i].astype(BF16)
        if i % 2 == 0:
            colscale = jnp.stack([jnp.full((1, d), q_scale, F32), jnp.ones((1, d), F32), jnp.ones((1, d), F32)])
            qk, v = _qkv_proj(x, norm_mix[i], sc1, sh1, w_in_b, colscale, qk_norm=False, v_width=HEAD_DIM)
            o = _sb_attention(qk, v, d)
        else:
            j = i // 2
            lambda_init = 0.8 - 0.6 * math.exp(-0.3 * i)
            reps = d // HEAD_DIM
            colscale = jnp.stack([
                jnp.tile(q_norm[j].astype(F32), reps)[None, :] * q_scale,
                jnp.tile(k_norm[j].astype(F32), reps)[None, :],
                jnp.ones((1, d), F32)])
            qk, v = _qkv_proj(x, norm_mix[i], sc1, sh1, w_in_b, colscale, qk_norm=True, v_width=2 * HEAD_DIM)
            lam_rows = jnp.stack([lambda_q1[j], lambda_k1[j], lambda_q2[j], lambda_k2[j]]).astype(F32)
            o = _da_attention(qk, v, lam_rows, subln[j].astype(F32), d, lambda_init)
        x = _out_proj(o, w_out[i].astype(BF16), x, g1)
        x = _moe_layer(x, norm_ffn[i], sc2, sh2, g2, w_router, router_bias, w_gate, w_up, w_down,
                       ws_gate_b, ws_up_b, ws_down_b, i)
    return x
```

```python
import functools
import math

import jax
import jax.numpy as jnp
from jax import lax
from jax.experimental import pallas as pl
from jax.experimental.pallas import tpu as pltpu

F32 = jnp.float32
BF16 = jnp.bfloat16
I32 = jnp.int32

LANES = 128
SUBLANES = 8
VMEM_LIMIT = 56 * 1024 * 1024

SB_HEADS = 16
DA_HEADS = 8
HEAD_DIM = 128
N_EXPERTS = 64
TOP_K = 6
N_GROUPS = 8
TOPK_GROUPS = 4
ROUTED_SCALE = 2.5
EPS = 1e-6
SUBLN_EPS = 1e-5
LOG2E = math.log2(math.e)

SB_TQ, SB_TK = 512, 256
DA_TQ, DA_TK = 256, 512
PROJ_TM = 512
ROUTE_TM = 512
MOE_BLK = 256
DISP_TM = 512
COMB_TM = 128


def _cparams(sem):
    return pltpu.CompilerParams(dimension_semantics=sem, vmem_limit_bytes=VMEM_LIMIT)


def _silu(x):
    return x * jax.nn.sigmoid(x)


def _dot(a, b):
    return jnp.dot(a, b, preferred_element_type=F32)


def _dot_nt(a, b):
    return lax.dot_general(a, b, (((1,), (1,)), ((), ())), preferred_element_type=F32)


def _store_slabs(ref, base, rows, vals):
    slab = vals.shape[1] // LANES
    for s in range(slab):
        ref[pl.ds(base + s, rows, stride=slab), :] = vals[:, s * LANES:(s + 1) * LANES]


def _load_slabs(ref, base, rows, slab, pitch=None):
    return [ref[pl.ds(base + s, rows, stride=pitch or slab), :] for s in range(slab)]


def _mod_kernel(c_ref, w_ref, b_ref, o_ref):
    cond = _silu(c_ref[...])
    o_ref[0] = _dot(cond.astype(BF16), w_ref[0].astype(BF16)) + b_ref[0]


def _modulation(c, w_mod, b_mod):
    depth, d, n = w_mod.shape
    b = c.shape[0]
    rows = -(-b // SUBLANES) * SUBLANES
    cp = jnp.pad(c, ((0, rows - b), (0, 0)))
    tn = 1024
    out = pl.pallas_call(
        _mod_kernel,
        grid=(depth, n // tn),
        in_specs=[
            pl.BlockSpec((rows, d), lambda l, j: (0, 0)),
            pl.BlockSpec((1, d, tn), lambda l, j: (l, 0, j)),
            pl.BlockSpec((1, 1, tn), lambda l, j: (l, 0, j)),
        ],
        out_specs=pl.BlockSpec((1, rows, tn), lambda l, j: (l, 0, j)),
        out_shape=jax.ShapeDtypeStruct((depth, rows, n), F32),
        compiler_params=_cparams(("arbitrary", "arbitrary")),
        name="adaln_mod",
    )(cp, w_mod, b_mod.reshape(depth, 1, n))
    return out[:, :b]


def _norm_mod(x, g, sc, sh):
    ms = jnp.mean(x * x, axis=-1, keepdims=True)
    return x * lax.rsqrt(ms + EPS) * g * (1.0 + sc) + sh


def _qkv_kernel(x_ref, g_ref, sc_ref, sh_ref, w_ref, cs_ref, qk_ref, v_ref, h_ref, *, qk_norm, v_width):
    j = pl.program_id(2)

    @pl.when(j == 0)
    def _():
        h_ref[...] = _norm_mod(x_ref[0], g_ref[...], sc_ref[0], sh_ref[0]).astype(BF16)

    acc = _dot(h_ref[...], w_ref[...])
    n_chunks = acc.shape[1] // HEAD_DIM

    @pl.when(j < 2)
    def _():
        for c in range(n_chunks):
            a = acc[:, c * HEAD_DIM:(c + 1) * HEAD_DIM]
            if qk_norm:
                a = a * lax.rsqrt(jnp.mean(a * a, axis=-1, keepdims=True) + EPS)
            a = a * cs_ref[0][:, c * HEAD_DIM:(c + 1) * HEAD_DIM]
            qk_ref[0, c] = a.astype(BF16)

    @pl.when(j == 2)
    def _():
        for c in range(acc.shape[1] // v_width):
            v_ref[0, c] = acc[:, c * v_width:(c + 1) * v_width].astype(BF16)


def _qkv_proj(x, g, sc, sh, w_bf16, colscale, *, qk_norm, v_width):
    b, s, d = x.shape
    tm = PROJ_TM
    n_ch = d // HEAD_DIM
    kern = functools.partial(_qkv_kernel, qk_norm=qk_norm, v_width=v_width)
    return pl.pallas_call(
        kern,
        grid=(b, s // tm, 3),
        in_specs=[
            pl.BlockSpec((1, tm, d), lambda bi, i, j: (bi, i, 0)),
            pl.BlockSpec((1, d), lambda bi, i, j: (0, 0)),
            pl.BlockSpec((1, 1, d), lambda bi, i, j: (bi, 0, 0)),
            pl.BlockSpec((1, 1, d), lambda bi, i, j: (bi, 0, 0)),
            pl.BlockSpec((d, d), lambda bi, i, j: (0, j)),
            pl.BlockSpec((1, 1, d), lambda bi, i, j: (j, 0, 0)),
        ],
        out_specs=[
            pl.BlockSpec((1, n_ch, tm, HEAD_DIM), lambda bi, i, j: (bi, jnp.minimum(j, 1), i, 0)),
            pl.BlockSpec((1, d // v_width, tm, v_width), lambda bi, i, j: (bi, 0, i, 0)),
        ],
        out_shape=[
            jax.ShapeDtypeStruct((b, 2 * n_ch, s, HEAD_DIM), BF16),
            jax.ShapeDtypeStruct((b, d // v_width, s, v_width), BF16),
        ],
        scratch_shapes=[pltpu.VMEM((tm, d), BF16)],
        compiler_params=_cparams(("arbitrary", "arbitrary", "arbitrary")),
        name="norm_qkv_proj",
    )(x, g.reshape(1, d), sc, sh, w_bf16, colscale)


def _sb_kernel(q_ref, k_ref, v_ref, u_ref, o_ref, *, tq, tk):
    i = pl.program_id(2)
    q = q_ref[0, 0]
    per = tq // tk
    rows = i * tq + lax.broadcasted_iota(I32, (tq, 1), 0)
    u = u_ref[...]

    def chunks_of(g):
        return [per * (i - g) + (per - 1 - r) for r in range(per)]

    def weights(c, r0, run, masked):
        start = pl.multiple_of(c * tk, tk)
        z = _dot_nt(q[r0:], k_ref[0, 0, pl.ds(start, tk), :])
        softplus = jnp.maximum(z, 0.0) + jnp.log(1.0 + jnp.exp2(-jnp.abs(z))) * LOG2E
        if masked:
            earlier = (start + lax.broadcasted_iota(I32, (1, tk), 1)) < rows[r0:]
            softplus = jnp.where(earlier, softplus, 0.0)
        between = _dot(softplus.astype(BF16), u) + run[r0:]
        a = jnp.exp2((z - softplus) + between)
        if masked:
            a = jnp.where(earlier, a, 0.0)
        dec = jnp.sum(softplus, axis=-1, keepdims=True)
        run = jnp.concatenate([run[:r0], run[r0:] - dec], axis=0) if r0 else run - dec
        return a.astype(BF16), run

    def values(c, r0, a, acc):
        pv = _dot(a, v_ref[0, 0, pl.ds(pl.multiple_of(c * tk, tk), tk), :])
        return jnp.concatenate([acc[:r0], acc[r0:] + pv], axis=0) if r0 else acc + pv

    def group(chunks, run, acc, masked):
        ws = []
        for c, r0 in chunks:
            a, run = weights(c, r0, run, masked)
            ws.append(a)
        for (c, r0), a in zip(chunks, ws):
            acc = values(c, r0, a, acc)
        return run, acc

    run, acc = group([(per * i + m, m * tk) for m in reversed(range(per))],
                     jnp.zeros((tq, 1), F32), jnp.zeros((tq, HEAD_DIM), F32), True)
    _, acc = lax.fori_loop(1, i + 1, lambda g, cr: group([(c, 0) for c in chunks_of(g)], cr[0], cr[1], False),
                           (run, acc))
    o_ref[0] = acc.astype(BF16)


def _sb_attention(qk, v, d):
    b, _, s, _ = qk.shape
    tq, tk = SB_TQ, SB_TK
    heads = SB_HEADS
    jj = lax.broadcasted_iota(I32, (tk, tk), 0)
    ss = lax.broadcasted_iota(I32, (tk, tk), 1)
    u = jnp.where(jj > ss, -1.0, 0.0).astype(BF16)
    kern = functools.partial(_sb_kernel, tq=tq, tk=tk)
    return pl.pallas_call(
        kern,
        grid=(b, heads, s // tq),
        in_specs=[
            pl.BlockSpec((1, 1, tq, HEAD_DIM), lambda bi, h, i: (bi, h, i, 0)),
            pl.BlockSpec((1, 1, s, HEAD_DIM), lambda bi, h, i: (bi, heads + h, 0, 0)),
            pl.BlockSpec((1, 1, s, HEAD_DIM), lambda bi, h, i: (bi, h, 0, 0)),
            pl.BlockSpec((tk, tk), lambda bi, h, i: (0, 0)),
        ],
        out_specs=pl.BlockSpec((1, tq, HEAD_DIM), lambda bi, h, i: (bi, i, h)),
        out_shape=jax.ShapeDtypeStruct((b, s, d), BF16),
        compiler_params=_cparams(("arbitrary", "arbitrary", "arbitrary")),
        name="stickbreak_attn",
    )(qk, qk, v, u)


def _da_kernel(q_ref, k_ref, v_ref, slope_ref, lam_ref, subln_ref, o_ref, *, tq, tk, lambda_init):
    i = pl.program_id(2)
    row0 = i * tq
    rows = row0 + lax.broadcasted_iota(I32, (tq, 1), 0)
    slope = slope_ref[0]
    qs = (q_ref[0, 0], q_ref[0, 1])
    dv = v_ref.shape[-1]

    def chunk(start, width, carry, masked):
        start = pl.multiple_of(start, tq)
        cols = start + lax.broadcasted_iota(I32, (1, width), 1)
        key_bias = slope * (cols - row0).astype(F32)
        v = v_ref[0, 0, pl.ds(start, width), :]
        new = []
        for m in range(2):
            mx, den, acc = carry[3 * m:3 * m + 3]
            k = k_ref[0, m, pl.ds(start, width), :]
            z = _dot_nt(qs[m], k) + key_bias
            if masked:
                z = jnp.where(cols <= rows, z, -jnp.inf)
            mx_new = jnp.maximum(mx, jnp.max(z, axis=-1, keepdims=True))
            alpha = jnp.exp2(mx - mx_new)
            p = jnp.exp2(z - mx_new)
            den = alpha * den + jnp.sum(p, axis=-1, keepdims=True)
            acc = alpha * acc + _dot(p.astype(BF16), v)
            new += [mx_new, den, acc]
        return tuple(new)

    init = []
    for _ in range(2):
        init += [jnp.full((tq, 1), -jnp.inf, F32), jnp.zeros((tq, 1), F32), jnp.zeros((tq, dv), F32)]
    n_below = row0 // tk
    carry = lax.fori_loop(0, n_below, lambda c, cr: chunk(c * tk, tk, cr, False), tuple(init))
    carry = chunk(n_below * tk, tk, carry, True)
    _, l1, a1, _, l2, a2 = carry

    lv = lam_ref[...]
    s1 = jnp.sum(lv[0:1] * lv[1:2], axis=-1, keepdims=True)
    s2 = jnp.sum(lv[2:3] * lv[3:4], axis=-1, keepdims=True)
    lam = jnp.exp(s1) - jnp.exp(s2) + lambda_init
    o = a1 / l1 - lam * (a2 / l2)
    o = o * lax.rsqrt(jnp.mean(o * o, axis=-1, keepdims=True) + SUBLN_EPS) * subln_ref[...]
    o_ref[0] = (o * (1.0 - lambda_init)).astype(BF16)


def _da_attention(qk, v, lam_rows, subln, d, lambda_init):
    b, _, s, _ = qk.shape
    tq, tk = DA_TQ, DA_TK
    heads = DA_HEADS
    dv = v.shape[-1]
    slopes = jnp.asarray(
        [2.0 ** (-8.0 * (h + 1) / heads) * LOG2E for h in range(heads)], F32).reshape(heads, 1, 1)
    kern = functools.partial(_da_kernel, tq=tq, tk=tk, lambda_init=lambda_init)
    return pl.pallas_call(
        kern,
        grid=(b, heads, s // tq),
        in_specs=[
            pl.BlockSpec((1, 2, tq, HEAD_DIM), lambda bi, h, i: (bi, h, i, 0)),
            pl.BlockSpec((1, 2, s, HEAD_DIM), lambda bi, h, i: (bi, heads + h, 0, 0)),
            pl.BlockSpec((1, 1, s, dv), lambda bi, h, i: (bi, h, 0, 0)),
            pl.BlockSpec((1, 1, 1), lambda bi, h, i: (h, 0, 0)),
            pl.BlockSpec((4, HEAD_DIM), lambda bi, h, i: (0, 0)),
            pl.BlockSpec((1, dv), lambda bi, h, i: (0, 0)),
        ],
        out_specs=pl.BlockSpec((1, tq, dv), lambda bi, h, i: (bi, i, h)),
        out_shape=jax.ShapeDtypeStruct((b, s, d), BF16),
        compiler_params=_cparams(("arbitrary", "arbitrary", "arbitrary")),
        name="diff_attn",
    )(qk, qk, v, slopes, lam_rows, subln.reshape(1, dv))


def _outproj_kernel(o_ref, w_ref, x_ref, g_ref, y_ref):
    y_ref[0] = x_ref[0] + g_ref[0] * _dot(o_ref[0], w_ref[...])


def _out_proj(o, w_bf16, x, gate):
    b, s, d = x.shape
    tm = PROJ_TM
    return pl.pallas_call(
        _outproj_kernel,
        grid=(b, s // tm),
        in_specs=[
            pl.BlockSpec((1, tm, d), lambda bi, i: (bi, i, 0)),
            pl.BlockSpec((d, d), lambda bi, i: (0, 0)),
            pl.BlockSpec((1, tm, d), lambda bi, i: (bi, i, 0)),
            pl.BlockSpec((1, 1, d), lambda bi, i: (bi, 0, 0)),
        ],
        out_specs=pl.BlockSpec((1, tm, d), lambda bi, i: (bi, i, 0)),
        out_shape=jax.ShapeDtypeStruct((b, s, d), F32),
        compiler_params=_cparams(("arbitrary", "arbitrary")),
        name="out_proj_residual",
    )(o, w_bf16, x, gate)


def _router_kernel(x_ref, g_ref, sc_ref, sh_ref, wr_ref, bias_ref, u_ref,
                   h_ref, slab_ref, te_ref, gate_ref, rank_ref, cnt_ref, run_ref, *, tm):
    i = pl.program_id(0)
    per_group = N_EXPERTS // N_GROUPS

    @pl.when(i == 0)
    def _():
        run_ref[...] = jnp.zeros_like(run_ref)

    h = _norm_mod(x_ref[...], g_ref[...], sc_ref[0], sh_ref[0])
    h_ref[...] = h.astype(BF16)
    _store_slabs(slab_ref, 0, tm, h)
    logits = lax.dot_general(wr_ref[0], h, (((1,), (1,)), ((), ())),
                             precision=lax.Precision.HIGHEST, preferred_element_type=F32)
    scores = jax.nn.sigmoid(logits)
    biased = scores + bias_ref[0]
    neg = -jnp.inf
    sub = lax.broadcasted_iota(I32, (per_group, tm), 0)

    def first_max(vals, idx, sentinel):
        m = jnp.max(vals, axis=0, keepdims=True)
        return m, jnp.min(jnp.where(vals == m, idx, sentinel), axis=0, keepdims=True)

    sc_g = [scores[g * per_group:(g + 1) * per_group] for g in range(N_GROUPS)]
    bi_g = [biased[g * per_group:(g + 1) * per_group] for g in range(N_GROUPS)]

    gscore = jnp.zeros((N_GROUPS, tm), F32)
    for g in range(N_GROUPS):
        m1, i1 = first_max(bi_g[g], sub, per_group)
        m2 = jnp.max(jnp.where(sub == i1, neg, bi_g[g]), axis=0, keepdims=True)
        gscore = jnp.where(sub == g, m1 + m2, gscore)

    keep = jnp.zeros((N_GROUPS, tm), F32)
    cur = gscore
    for _ in range(TOPK_GROUPS):
        _, gi = first_max(cur, sub, N_GROUPS)
        hit = sub == gi
        keep = jnp.where(hit, 1.0, keep)
        cur = jnp.where(hit, neg, cur)

    masked = []
    for g in range(N_GROUPS):
        kg = jnp.max(jnp.where(sub == g, keep, 0.0), axis=0, keepdims=True)
        masked.append(jnp.where(kg > 0.0, bi_g[g], neg))
    eidx = [sub + g * per_group for g in range(N_GROUPS)]
    sel = [jnp.zeros((per_group, tm), F32) for _ in range(N_GROUPS)]

    top_e, gates = [], []
    for _ in range(TOP_K):
        m = masked[0].max(axis=0, keepdims=True)
        for g in range(1, N_GROUPS):
            m = jnp.maximum(m, jnp.max(masked[g], axis=0, keepdims=True))
        ei = jnp.full((1, tm), N_EXPERTS, I32)
        for g in range(N_GROUPS):
            ei = jnp.minimum(ei, jnp.min(jnp.where(masked[g] == m, eidx[g], N_EXPERTS), axis=0, keepdims=True))
        gt = jnp.zeros((1, tm), F32)
        for g in range(N_GROUPS):
            hit = eidx[g] == ei
            gt = gt + jnp.sum(jnp.where(hit, sc_g[g], 0.0), axis=0, keepdims=True)
            masked[g] = jnp.where(hit, neg, masked[g])
            sel[g] = jnp.where(hit, 1.0, sel[g])
        top_e.append(ei)
        gates.append(gt)

    gsum = gates[0]
    for gt in gates[1:]:
        gsum = gsum + gt

    sel_all = jnp.concatenate(sel, axis=0)
    before = _dot(sel_all.astype(BF16), u_ref[...]) + run_ref[...]
    run_new = run_ref[...] + jnp.sum(sel_all, axis=1, keepdims=True)
    run_ref[...] = run_new
    cnt_ref[...] = jnp.broadcast_to(run_new, cnt_ref.shape).astype(I32)

    te_ref[...] = jnp.zeros_like(te_ref)
    gate_ref[...] = jnp.zeros_like(gate_ref)
    rank_ref[...] = jnp.zeros_like(rank_ref)
    for j in range(TOP_K):
        rk = jnp.zeros((1, tm), F32)
        for g in range(N_GROUPS):
            rk = rk + jnp.sum(jnp.where(eidx[g] == top_e[j], before[g * per_group:(g + 1) * per_group], 0.0),
                              axis=0, keepdims=True)
        te_ref[j:j + 1, :] = top_e[j]
        gate_ref[j:j + 1, :] = gates[j] / gsum * ROUTED_SCALE
        rank_ref[j:j + 1, :] = rk.astype(I32)


def _route(x2, g, sc, sh, w_router, bias, layer, seq):
    t, d = x2.shape
    tm = ROUTE_TM
    e = N_EXPERTS
    slab = d // LANES
    jj = lax.broadcasted_iota(I32, (tm, tm), 0)
    ss = lax.broadcasted_iota(I32, (tm, tm), 1)
    u = (jj < ss).astype(BF16)
    per_seq = seq // tm
    kern = functools.partial(_router_kernel, tm=tm)
    rows = SUBLANES
    depth = w_router.shape[0]
    h, h_slabs, te, gate, rank, cnt = pl.pallas_call(
        kern,
        grid=(t // tm,),
        in_specs=[
            pl.BlockSpec((tm, d), lambda i: (i, 0)),
            pl.BlockSpec((1, d), lambda i: (0, 0)),
            pl.BlockSpec((1, 1, d), lambda i: (i // per_seq, 0, 0)),
            pl.BlockSpec((1, 1, d), lambda i: (i // per_seq, 0, 0)),
            pl.BlockSpec((1, e, d), lambda i: (layer, 0, 0)),
            pl.BlockSpec((1, e, 1), lambda i: (layer, 0, 0)),
            pl.BlockSpec((tm, tm), lambda i: (0, 0)),
        ],
        out_specs=[
            pl.BlockSpec((tm, d), lambda i: (i, 0)),
            pl.BlockSpec((tm * slab, LANES), lambda i: (i, 0)),
            pl.BlockSpec((rows, tm), lambda i: (0, i)),
            pl.BlockSpec((rows, tm), lambda i: (0, i)),
            pl.BlockSpec((rows, tm), lambda i: (0, i)),
            pl.BlockSpec((e, LANES), lambda i: (0, 0)),
        ],
        out_shape=[
            jax.ShapeDtypeStruct((t, d), BF16),
            jax.ShapeDtypeStruct((t * slab, LANES), F32),
            jax.ShapeDtypeStruct((rows, t), I32),
            jax.ShapeDtypeStruct((rows, t), F32),
            jax.ShapeDtypeStruct((rows, t), I32),
            jax.ShapeDtypeStruct((e, LANES), I32),
        ],
        scratch_shapes=[pltpu.VMEM((e, 1), F32)],
        compiler_params=_cparams(("arbitrary",)),
        name="norm_router_topk",
    )(x2, g.reshape(1, d), sc, sh, w_router, bias.reshape(depth, e, 1), u)
    return h, h_slabs, te[:TOP_K], gate[:TOP_K], rank[:TOP_K], cnt[:, 0]


def _dispatch_kernel(slot_ref, pe_ref, nu_ref, h_ref, hb_ref, wg_ref, wu_ref, wd_ref, xs_hbm, sh_ref,
                     zbuf, sem, zsem, *, tm, blk, n_blocks, slab):
    i = pl.program_id(0)
    n_used = nu_ref[0]
    group = 8

    def zero_copy(block):
        start = pl.multiple_of(block * (blk * slab), blk * slab)
        return pltpu.make_async_copy(zbuf, xs_hbm.at[pl.ds(start, blk * slab)], zsem)

    def expert_has_rows(e):
        return pe_ref[e] > (pe_ref[e - 1] if e else 0)

    @pl.when(i == 0)
    def _():
        zbuf[...] = jnp.zeros_like(zbuf)
        for phase in ("start", "wait"):
            for e in range(N_EXPERTS):
                @pl.when(expert_has_rows(e))
                def _():
                    cp = zero_copy(pe_ref[e] // blk - 1)
                    cp.start() if phase == "start" else cp.wait()

            def tail(bk, carry):
                cp = zero_copy(bk)
                cp.start() if phase == "start" else cp.wait()
                return carry

            lax.fori_loop(n_used, n_blocks, tail, 0)

    base = i * (TOP_K * tm)
    for j in range(TOP_K):
        def body(g, carry):
            for r in range(group):
                dst = pl.multiple_of(slot_ref[base + j * tm + g * group + r] * slab, slab)
                src = pl.multiple_of(g * (group * slab), group * slab) + r * slab
                pltpu.make_async_copy(h_ref.at[pl.ds(src, slab)], xs_hbm.at[pl.ds(dst, slab)], sem).start(
                    priority=r % 2)
            return carry

        lax.fori_loop(0, tm // group, body, 0)

    hb = hb_ref[...]
    mid = _silu(_dot(hb, wg_ref[0])) * _dot(hb, wu_ref[0])
    sh_ref[...] = _dot(mid.astype(BF16), wd_ref[0]).astype(BF16)

    for j in range(TOP_K):
        pltpu.make_async_copy(h_ref, xs_hbm.at[pl.ds(0, tm * slab)], sem).wait()


def _dispatch(h_slabs, h, slots_tiled, pad_end, n_used, n_blocks, slab, ws_gate, ws_up, ws_down, layer):
    t, d = h.shape
    f = ws_gate.shape[-1]
    tm = DISP_TM
    blk = MOE_BLK
    kern = functools.partial(_dispatch_kernel, tm=tm, blk=blk, n_blocks=n_blocks, slab=slab)
    grid_spec = pltpu.PrefetchScalarGridSpec(
        num_scalar_prefetch=3,
        grid=(t // tm,),
        in_specs=[
            pl.BlockSpec((tm * slab, LANES), lambda i, sl, pe, nu: (i, 0)),
            pl.BlockSpec((tm, d), lambda i, sl, pe, nu: (i, 0)),
            pl.BlockSpec((1, d, f), lambda i, sl, pe, nu: (layer, 0, 0)),
            pl.BlockSpec((1, d, f), lambda i, sl, pe, nu: (layer, 0, 0)),
            pl.BlockSpec((1, f, d), lambda i, sl, pe, nu: (layer, 0, 0)),
        ],
        out_specs=[
            pl.BlockSpec(memory_space=pl.ANY),
            pl.BlockSpec((tm, d), lambda i, sl, pe, nu: (i, 0)),
        ],
        scratch_shapes=[
            pltpu.VMEM((blk * slab, LANES), F32),
            pltpu.SemaphoreType.DMA,
            pltpu.SemaphoreType.DMA,
        ],
    )
    return pl.pallas_call(
        kern,
        grid_spec=grid_spec,
        out_shape=[
            jax.ShapeDtypeStruct((n_blocks * blk * slab, LANES), F32),
            jax.ShapeDtypeStruct((t, d), BF16),
        ],
        compiler_params=_cparams(("arbitrary",)),
        name="dispatch_rows",
    )(slots_tiled, pad_end, n_used, h_slabs, h, ws_gate, ws_up, ws_down)


def _expert_kernel(be_ref, ne_ref, nu_ref, x_ref, wg_hbm, wu_hbm, wd_hbm, y_ref,
                   wg32, wu32, wd32, wgb, wub, wdb, sem, *, blk, slab, layer):
    i = pl.program_id(0)
    n_used = nu_ref[0]

    def weight_copies(e):
        return [pltpu.make_async_copy(src.at[layer, e], dst, sem)
                for src, dst in ((wg_hbm, wg32), (wu_hbm, wu32), (wd_hbm, wd32))]

    @pl.when(i == 0)
    def _():
        for cp in weight_copies(be_ref[0]):
            cp.start()

    @pl.when(i < n_used)
    def _():
        prev = be_ref[jnp.maximum(i - 1, 0)]

        @pl.when((i == 0) | (be_ref[i] != prev))
        def _():
            for cp in weight_copies(be_ref[i]):
                cp.wait()
            wgb[...] = wg32[...].astype(BF16)
            wub[...] = wu32[...].astype(BF16)
            wdb[...] = wd32[...].astype(BF16)
            nxt = ne_ref[i]

            @pl.when(nxt >= 0)
            def _():
                for cp in weight_copies(nxt):
                    cp.start()

        x = jnp.concatenate([c.astype(BF16) for c in _load_slabs(x_ref, 0, blk, slab)], axis=1)
        mid = _silu(_dot(x, wgb[...])) * _dot(x, wub[...])
        _store_slabs(y_ref, 0, blk, _dot(mid.astype(BF16), wdb[...]))

    @pl.when(i >= n_used)
    def _():
        y_ref[...] = jnp.zeros_like(y_ref)


def _experts(xs, w_gate, w_up, w_down, layer, block_e, next_expert, n_used, slab):
    d, f = w_gate.shape[-2:]
    blk = MOE_BLK
    n_blocks = xs.shape[0] // (blk * slab)
    kern = functools.partial(_expert_kernel, blk=blk, slab=slab, layer=layer)
    grid_spec = pltpu.PrefetchScalarGridSpec(
        num_scalar_prefetch=3,
        grid=(n_blocks,),
        in_specs=[
            pl.BlockSpec((blk * slab, LANES), lambda i, be, ne, nu: (jnp.minimum(i, nu[0] - 1), 0)),
            pl.BlockSpec(memory_space=pl.ANY),
            pl.BlockSpec(memory_space=pl.ANY),
            pl.BlockSpec(memory_space=pl.ANY),
        ],
        out_specs=pl.BlockSpec((blk * slab, LANES), lambda i, be, ne, nu: (i, 0)),
        scratch_shapes=[
            pltpu.VMEM((d, f), F32),
            pltpu.VMEM((d, f), F32),
            pltpu.VMEM((f, d), F32),
            pltpu.VMEM((d, f), BF16),
            pltpu.VMEM((d, f), BF16),
            pltpu.VMEM((f, d), BF16),
            pltpu.SemaphoreType.DMA,
        ],
    )
    return pl.pallas_call(
        kern,
        grid_spec=grid_spec,
        out_shape=jax.ShapeDtypeStruct(xs.shape, F32),
        compiler_params=_cparams(("arbitrary",)),
        name="routed_experts",
    )(block_e, next_expert, n_used, xs, w_gate, w_up, w_down)


def _combine_kernel(slot_ref, y_hbm, sh_ref, x_ref, g_ref, gw_ref, o_ref, ybuf, sem, *, tm, slab, pitch):
    i = pl.program_id(0)
    n = pl.num_programs(0)
    rows = TOP_K * tm
    group = 8

    def issue(tile, slot):
        base = tile * rows

        def body(g, carry):
            for r in range(group):
                src = pl.multiple_of(slot_ref[base + g * group + r] * slab, slab)
                dst = pl.multiple_of((slot * rows + g * group) * pitch, group * pitch) + r * pitch
                pltpu.make_async_copy(y_hbm.at[pl.ds(src, slab)], ybuf.at[pl.ds(dst, slab)], sem.at[slot]).start(
                    priority=r % 2)
            return carry

        lax.fori_loop(0, rows // group, body, 0)

    @pl.when(i == 0)
    def _():
        issue(0, 0)

    @pl.when(i + 1 < n)
    def _():
        issue(i + 1, (i + 1) % 2)

    slot = i % 2
    buf0 = pl.multiple_of(slot * (rows * pitch), rows * pitch)
    pltpu.make_async_copy(y_hbm.at[pl.ds(0, rows * slab)], ybuf.at[pl.ds(0, rows * slab)], sem.at[slot]).wait()
    gw = gw_ref[...]
    gwb = [jnp.broadcast_to(gw[:, j:j + 1], (tm, LANES)) for j in range(TOP_K)]
    chunks = [jnp.zeros((tm, LANES), F32) for _ in range(slab)]
    for j in range(TOP_K):
        for s, y in enumerate(_load_slabs(ybuf, buf0 + j * (tm * pitch), tm, slab, pitch)):
            chunks[s] = chunks[s] + gwb[j] * y
    routed = jnp.concatenate(chunks, axis=1)
    o_ref[...] = x_ref[...] + g_ref[0] * (sh_ref[...].astype(F32) + routed)


def _combine(slots_tiled, y_slots, shared, x2, gate, gate_w, seq, slab):
    t, d = x2.shape
    tm = COMB_TM
    per_seq = seq // tm
    pitch = slab + SUBLANES
    kern = functools.partial(_combine_kernel, tm=tm, slab=slab, pitch=pitch)
    grid_spec = pltpu.PrefetchScalarGridSpec(
        num_scalar_prefetch=1,
        grid=(t // tm,),
        in_specs=[
            pl.BlockSpec(memory_space=pl.ANY),
            pl.BlockSpec((tm, d), lambda i, sl: (i, 0)),
            pl.BlockSpec((tm, d), lambda i, sl: (i, 0)),
            pl.BlockSpec((1, 1, d), lambda i, sl: (i // per_seq, 0, 0)),
            pl.BlockSpec((tm, SUBLANES), lambda i, sl: (i, 0)),
        ],
        out_specs=pl.BlockSpec((tm, d), lambda i, sl: (i, 0)),
        scratch_shapes=[
            pltpu.VMEM((2 * TOP_K * tm * pitch, LANES), F32),
            pltpu.SemaphoreType.DMA((2,)),
        ],
    )
    return pl.pallas_call(
        kern,
        grid_spec=grid_spec,
        out_shape=jax.ShapeDtypeStruct((t, d), F32),
        compiler_params=_cparams(("arbitrary",)),
        name="gather_combine",
    )(slots_tiled, y_slots, shared, x2, gate, gate_w)


def _tile_slots(slot, tm):
    k, t = slot.shape
    return slot.reshape(k, t // tm, tm).transpose(1, 0, 2).reshape(-1)


def _moe_layer(x, g, sc, sh, gate, w_router, router_bias, w_gate, w_up, w_down,
               ws_gate_b, ws_up_b, ws_down_b, layer):
    b, s, d = x.shape
    t = b * s
    x2 = x.reshape(t, d)
    slab = d // LANES
    h, h_slabs, top_e, gates, rank, counts = _route(x2, g, sc, sh, w_router, router_bias, layer, s)

    blk = MOE_BLK
    n_blocks = -(-(t * TOP_K) // blk) + N_EXPERTS
    padded = (counts + blk - 1) // blk * blk
    pad_end = jnp.cumsum(padded).astype(I32)
    pad_start = pad_end - padded
    experts = jnp.arange(N_EXPERTS, dtype=I32)
    slot = jnp.sum(jnp.where(top_e[..., None] == experts, pad_start, 0), axis=-1) + rank
    block_first = jnp.arange(n_blocks, dtype=I32) * blk
    block_e = jnp.minimum(jnp.sum((pad_end[None, :] <= block_first[:, None]).astype(I32), axis=1), N_EXPERTS - 1)
    n_used = pad_end[-1:] // blk
    has_rows = padded > 0
    later = jnp.where(has_rows[None, :] & (experts[None, :] > experts[:, None]), experts[None, :], N_EXPERTS)
    following = jnp.min(later, axis=1)
    following = jnp.where(following < N_EXPERTS, following, -1)
    block_hot = block_e[:, None] == experts[None, :]
    next_expert = jnp.sum(jnp.where(block_hot, following[None, :], 0), axis=1).astype(I32)

    xs, shared = _dispatch(h_slabs, h, _tile_slots(slot, DISP_TM), pad_end, n_used, n_blocks, slab,
                           ws_gate_b, ws_up_b, ws_down_b, layer)
    y_slots = _experts(xs, w_gate, w_up, w_down, layer, block_e, next_expert, n_used, slab)
    gate_w = jnp.pad(gates.T, ((0, 0), (0, SUBLANES - TOP_K)))
    out = _combine(_tile_slots(slot, COMB_TM), y_slots, shared, x2, gate, gate_w, s, slab)
    return out.reshape(b, s, d)


def kernel(x, c, w_mod, b_mod, norm_mix, norm_ffn, w_in, w_out, q_norm, k_norm, lambda_q1, lambda_k1,
           lambda_q2, lambda_k2, subln, w_router, router_bias, w_gate, w_up, w_down, ws_gate, ws_up, ws_down):
    b, s, d = x.shape
    depth = w_mod.shape[0]
    assert d == SB_HEADS * HEAD_DIM == 2 * DA_HEADS * HEAD_DIM
    assert s % max(SB_TQ, SB_TK, DA_TQ, DA_TK, PROJ_TM, ROUTE_TM, DISP_TM) == 0
    assert SB_TQ % SB_TK == 0 and DA_TK % DA_TQ == 0
    q_scale = HEAD_DIM ** -0.5 * LOG2E

    mod = _modulation(c, w_mod, b_mod)
    ws_gate_b, ws_up_b, ws_down_b = ws_gate.astype(BF16), ws_up.astype(BF16), ws_down.astype(BF16)
    for i in range(depth):
        sh1, sc1, g1, sh2, sc2, g2 = [m.reshape(b, 1, d) for m in jnp.split(mod[i], 6, axis=-1)]
        w_in_b = w_in[i].astype(BF16)
        if i % 2 == 0:
            colscale = jnp.stack([jnp.full((1, d), q_scale, F32), jnp.ones((1, d), F32), jnp.ones((1, d), F32)])
            qk, v = _qkv_proj(x, norm_mix[i], sc1, sh1, w_in_b, colscale, qk_norm=False, v_width=HEAD_DIM)
            o = _sb_attention(qk, v, d)
        else:
            j = i // 2
            lambda_init = 0.8 - 0.6 * math.exp(-0.3 * i)
            reps = d // HEAD_DIM
            colscale = jnp.stack([
                jnp.tile(q_norm[j].astype(F32), reps)[None, :] * q_scale,
                jnp.tile(k_norm[j].astype(F32), reps)[None, :],
                jnp.ones((1, d), F32)])
            qk, v = _qkv_proj(x, norm_mix[i], sc1, sh1, w_in_b, colscale, qk_norm=True, v_width=2 * HEAD_DIM)
            lam_rows = jnp.stack([lambda_q1[j], lambda_k1[j], lambda_q2[j], lambda_k2[j]]).astype(F32)
            o = _da_attention(qk, v, lam_rows, subln[j].astype(F32), d, lambda_init)
        x = _out_proj(o, w_out[i].astype(BF16), x, g1)
        x = _moe_layer(x, norm_ffn[i], sc2, sh2, g2, w_router, router_bias, w_gate, w_up, w_down,
                       ws_gate_b, ws_up_b, ws_down_b, i)
    return x
```

```python
import functools
import math

import jax
import jax.numpy as jnp
from jax import lax
from jax.experimental import pallas as pl
from jax.experimental.pallas import tpu as pltpu

F32 = jnp.float32
BF16 = jnp.bfloat16
I32 = jnp.int32

LANES = 128
SUBLANES = 8
VMEM_LIMIT = 56 * 1024 * 1024

SB_HEADS = 16
DA_HEADS = 8
HEAD_DIM = 128
N_EXPERTS = 64
TOP_K = 6
N_GROUPS = 8
TOPK_GROUPS = 4
ROUTED_SCALE = 2.5
EPS = 1e-6
SUBLN_EPS = 1e-5
LOG2E = math.log2(math.e)

SB_TQ, SB_TK = 512, 256
DA_TQ, DA_TK = 512, 1024
PROJ_TM = 512
ROUTE_TM = 512
MOE_BLK = 256
DISP_TM = 512
COMB_TM = 128


def _cparams(sem):
    return pltpu.CompilerParams(dimension_semantics=sem, vmem_limit_bytes=VMEM_LIMIT)


def _silu(x):
    return x * jax.nn.sigmoid(x)


def _dot(a, b):
    return jnp.dot(a, b, preferred_element_type=F32)


def _dot_nt(a, b):
    return lax.dot_general(a, b, (((1,), (1,)), ((), ())), preferred_element_type=F32)


def _store_slabs(ref, base, rows, vals):
    slab = vals.shape[1] // LANES
    for s in range(slab):
        ref[pl.ds(base + s, rows, stride=slab), :] = vals[:, s * LANES:(s + 1) * LANES]


def _load_slabs(ref, base, rows, slab, pitch=None):
    return [ref[pl.ds(base + s, rows, stride=pitch or slab), :] for s in range(slab)]


def _mod_kernel(c_ref, w_ref, b_ref, o_ref):
    cond = _silu(c_ref[...])
    o_ref[0] = _dot(cond.astype(BF16), w_ref[0].astype(BF16)) + b_ref[0]


def _modulation(c, w_mod, b_mod):
    depth, d, n = w_mod.shape
    b = c.shape[0]
    rows = -(-b // SUBLANES) * SUBLANES
    cp = jnp.pad(c, ((0, rows - b), (0, 0)))
    tn = 1024
    out = pl.pallas_call(
        _mod_kernel,
        grid=(depth, n // tn),
        in_specs=[
            pl.BlockSpec((rows, d), lambda l, j: (0, 0)),
            pl.BlockSpec((1, d, tn), lambda l, j: (l, 0, j)),
            pl.BlockSpec((1, 1, tn), lambda l, j: (l, 0, j)),
        ],
        out_specs=pl.BlockSpec((1, rows, tn), lambda l, j: (l, 0, j)),
        out_shape=jax.ShapeDtypeStruct((depth, rows, n), F32),
        compiler_params=_cparams(("arbitrary", "arbitrary")),
        name="adaln_mod",
    )(cp, w_mod, b_mod.reshape(depth, 1, n))
    return out[:, :b]


def _norm_mod(x, g, sc, sh):
    ms = jnp.mean(x * x, axis=-1, keepdims=True)
    return x * lax.rsqrt(ms + EPS) * g * (1.0 + sc) + sh


def _qkv_kernel(x_ref, g_ref, sc_ref, sh_ref, w_ref, cs_ref, qk_ref, v_ref, h_ref, *, qk_norm, v_width):
    j = pl.program_id(2)

    @pl.when(j == 0)
    def _():
        h_ref[...] = _norm_mod(x_ref[0], g_ref[...], sc_ref[0], sh_ref[0]).astype(BF16)

    acc = _dot(h_ref[...], w_ref[...])
    n_chunks = acc.shape[1] // HEAD_DIM

    @pl.when(j < 2)
    def _():
        for c in range(n_chunks):
            a = acc[:, c * HEAD_DIM:(c + 1) * HEAD_DIM]
            if qk_norm:
                a = a * lax.rsqrt(jnp.mean(a * a, axis=-1, keepdims=True) + EPS)
            a = a * cs_ref[0][:, c * HEAD_DIM:(c + 1) * HEAD_DIM]
            qk_ref[0, c] = a.astype(BF16)

    @pl.when(j == 2)
    def _():
        for c in range(acc.shape[1] // v_width):
            v_ref[0, c] = acc[:, c * v_width:(c + 1) * v_width].astype(BF16)


def _qkv_proj(x, g, sc, sh, w_bf16, colscale, *, qk_norm, v_width):
    b, s, d = x.shape
    tm = PROJ_TM
    n_ch = d // HEAD_DIM
    kern = functools.partial(_qkv_kernel, qk_norm=qk_norm, v_width=v_width)
    return pl.pallas_call(
        kern,
        grid=(b, s // tm, 3),
        in_specs=[
            pl.BlockSpec((1, tm, d), lambda bi, i, j: (bi, i, 0)),
            pl.BlockSpec((1, d), lambda bi, i, j: (0, 0)),
            pl.BlockSpec((1, 1, d), lambda bi, i, j: (bi, 0, 0)),
            pl.BlockSpec((1, 1, d), lambda bi, i, j: (bi, 0, 0)),
            pl.BlockSpec((d, d), lambda bi, i, j: (0, j)),
            pl.BlockSpec((1, 1, d), lambda bi, i, j: (j, 0, 0)),
        ],
        out_specs=[
            pl.BlockSpec((1, n_ch, tm, HEAD_DIM), lambda bi, i, j: (bi, jnp.minimum(j, 1), i, 0)),
            pl.BlockSpec((1, d // v_width, tm, v_width), lambda bi, i, j: (bi, 0, i, 0)),
        ],
        out_shape=[
            jax.ShapeDtypeStruct((b, 2 * n_ch, s, HEAD_DIM), BF16),
            jax.ShapeDtypeStruct((b, d // v_width, s, v_width), BF16),
        ],
        scratch_shapes=[pltpu.VMEM((tm, d), BF16)],
        compiler_params=_cparams(("arbitrary", "arbitrary", "arbitrary")),
        name="norm_qkv_proj",
    )(x, g.reshape(1, d), sc, sh, w_bf16, colscale)


def _sb_kernel(q_ref, k_ref, v_ref, u_ref, o_ref, *, tq, tk):
    i = pl.program_id(2)
    q = q_ref[0, 0]
    per = tq // tk
    rows = i * tq + lax.broadcasted_iota(I32, (tq, 1), 0)
    u = u_ref[...]

    def chunks_of(g):
        return [per * (i - g) + (per - 1 - r) for r in range(per)]

    def weights(c, r0, run, masked):
        start = pl.multiple_of(c * tk, tk)
        z = _dot_nt(q[r0:], k_ref[0, 0, pl.ds(start, tk), :])
        softplus = jnp.maximum(z, 0.0) + jnp.log(1.0 + jnp.exp2(-jnp.abs(z))) * LOG2E
        if masked:
            earlier = (start + lax.broadcasted_iota(I32, (1, tk), 1)) < rows[r0:]
            softplus = jnp.where(earlier, softplus, 0.0)
        between = _dot(softplus.astype(BF16), u) + run[r0:]
        a = jnp.exp2((z - softplus) + between)
        if masked:
            a = jnp.where(earlier, a, 0.0)
        dec = jnp.sum(softplus, axis=-1, keepdims=True)
        run = jnp.concatenate([run[:r0], run[r0:] - dec], axis=0) if r0 else run - dec
        return a.astype(BF16), run

    def values(c, r0, a, acc):
        pv = _dot(a, v_ref[0, 0, pl.ds(pl.multiple_of(c * tk, tk), tk), :])
        return jnp.concatenate([acc[:r0], acc[r0:] + pv], axis=0) if r0 else acc + pv

    def group(chunks, run, acc, masked):
        ws = []
        for c, r0 in chunks:
            a, run = weights(c, r0, run, masked)
            ws.append(a)
        for (c, r0), a in zip(chunks, ws):
            acc = values(c, r0, a, acc)
        return run, acc

    run, acc = group([(per * i + m, m * tk) for m in reversed(range(per))],
                     jnp.zeros((tq, 1), F32), jnp.zeros((tq, HEAD_DIM), F32), True)
    _, acc = lax.fori_loop(1, i + 1, lambda g, cr: group([(c, 0) for c in chunks_of(g)], cr[0], cr[1], False),
                           (run, acc))
    o_ref[0] = acc.astype(BF16)


def _sb_attention(qk, v, d):
    b, _, s, _ = qk.shape
    tq, tk = SB_TQ, SB_TK
    heads = SB_HEADS
    jj = lax.broadcasted_iota(I32, (tk, tk), 0)
    ss = lax.broadcasted_iota(I32, (tk, tk), 1)
    u = jnp.where(jj > ss, -1.0, 0.0).astype(BF16)
    kern = functools.partial(_sb_kernel, tq=tq, tk=tk)
    return pl.pallas_call(
        kern,
        grid=(b, heads, s // tq),
        in_specs=[
            pl.BlockSpec((1, 1, tq, HEAD_DIM), lambda bi, h, i: (bi, h, i, 0)),
            pl.BlockSpec((1, 1, s, HEAD_DIM), lambda bi, h, i: (bi, heads + h, 0, 0)),
            pl.BlockSpec((1, 1, s, HEAD_DIM), lambda bi, h, i: (bi, h, 0, 0)),
            pl.BlockSpec((tk, tk), lambda bi, h, i: (0, 0)),
        ],
        out_specs=pl.BlockSpec((1, tq, HEAD_DIM), lambda bi, h, i: (bi, i, h)),
        out_shape=jax.ShapeDtypeStruct((b, s, d), BF16),
        compiler_params=_cparams(("arbitrary", "arbitrary", "arbitrary")),
        name="stickbreak_attn",
    )(qk, qk, v, u)


def _da_kernel(q_ref, k_ref, v_ref, slope_ref, lam_ref, subln_ref, o_ref, *, tq, tk, lambda_init):
    i = pl.program_id(2)
    row0 = i * tq
    rows = row0 + lax.broadcasted_iota(I32, (tq, 1), 0)
    slope = slope_ref[0]
    qs = (q_ref[0, 0], q_ref[0, 1])
    dv = v_ref.shape[-1]

    def chunk(start, width, carry, masked):
        start = pl.multiple_of(start, tq)
        cols = start + lax.broadcasted_iota(I32, (1, width), 1)
        key_bias = slope * (cols - row0).astype(F32)
        v = v_ref[0, 0, pl.ds(start, width), :]
        new = []
        for m in range(2):
            mx, den, acc = carry[3 * m:3 * m + 3]
            k = k_ref[0, m, pl.ds(start, width), :]
            z = _dot_nt(qs[m], k) + key_bias
            if masked:
                z = jnp.where(cols <= rows, z, -jnp.inf)
            mx_new = jnp.maximum(mx, jnp.max(z, axis=-1, keepdims=True))
            alpha = jnp.exp2(mx - mx_new)
            p = jnp.exp2(z - mx_new)
            den = alpha * den + jnp.sum(p, axis=-1, keepdims=True)
            acc = alpha * acc + _dot(p.astype(BF16), v)
            new += [mx_new, den, acc]
        return tuple(new)

    init = []
    for _ in range(2):
        init += [jnp.full((tq, 1), -jnp.inf, F32), jnp.zeros((tq, 1), F32), jnp.zeros((tq, dv), F32)]
    n_below = row0 // tk
    carry = lax.fori_loop(0, n_below, lambda c, cr: chunk(c * tk, tk, cr, False), tuple(init))
    carry = chunk(n_below * tk, tk, carry, True)
    _, l1, a1, _, l2, a2 = carry

    lv = lam_ref[...]
    s1 = jnp.sum(lv[0:1] * lv[1:2], axis=-1, keepdims=True)
    s2 = jnp.sum(lv[2:3] * lv[3:4], axis=-1, keepdims=True)
    lam = jnp.exp(s1) - jnp.exp(s2) + lambda_init
    o = a1 / l1 - lam * (a2 / l2)
    o = o * lax.rsqrt(jnp.mean(o * o, axis=-1, keepdims=True) + SUBLN_EPS) * subln_ref[...]
    o_ref[0] = (o * (1.0 - lambda_init)).astype(BF16)


def _da_attention(qk, v, lam_rows, subln, d, lambda_init):
    b, _, s, _ = qk.shape
    tq, tk = DA_TQ, DA_TK
    heads = DA_HEADS
    dv = v.shape[-1]
    slopes = jnp.asarray(
        [2.0 ** (-8.0 * (h + 1) / heads) * LOG2E for h in range(heads)], F32).reshape(heads, 1, 1)
    kern = functools.partial(_da_kernel, tq=tq, tk=tk, lambda_init=lambda_init)
    return pl.pallas_call(
        kern,
        grid=(b, heads, s // tq),
        in_specs=[
            pl.BlockSpec((1, 2, tq, HEAD_DIM), lambda bi, h, i: (bi, h, i, 0)),
            pl.BlockSpec((1, 2, s, HEAD_DIM), lambda bi, h, i: (bi, heads + h, 0, 0)),
            pl.BlockSpec((1, 1, s, dv), lambda bi, h, i: (bi, h, 0, 0)),
            pl.BlockSpec((1, 1, 1), lambda bi, h, i: (h, 0, 0)),
            pl.BlockSpec((4, HEAD_DIM), lambda bi, h, i: (0, 0)),
            pl.BlockSpec((1, dv), lambda bi, h, i: (0, 0)),
        ],
        out_specs=pl.BlockSpec((1, tq, dv), lambda bi, h, i: (bi, i, h)),
        out_shape=jax.ShapeDtypeStruct((b, s, d), BF16),
        compiler_params=_cparams(("arbitrary", "arbitrary", "arbitrary")),
        name="diff_attn",
    )(qk, qk, v, slopes, lam_rows, subln.reshape(1, dv))


def _outproj_kernel(o_ref, w_ref, x_ref, g_ref, y_ref):
    y_ref[0] = x_ref[0] + g_ref[0] * _dot(o_ref[0], w_ref[...])


def _out_proj(o, w_bf16, x, gate):
    b, s, d = x.shape
    tm = PROJ_TM
    return pl.pallas_call(
        _outproj_kernel,
        grid=(b, s // tm),
        in_specs=[
            pl.BlockSpec((1, tm, d), lambda bi, i: (bi, i, 0)),
            pl.BlockSpec((d, d), lambda bi, i: (0, 0)),
            pl.BlockSpec((1, tm, d), lambda bi, i: (bi, i, 0)),
            pl.BlockSpec((1, 1, d), lambda bi, i: (bi, 0, 0)),
        ],
        out_specs=pl.BlockSpec((1, tm, d), lambda bi, i: (bi, i, 0)),
        out_shape=jax.ShapeDtypeStruct((b, s, d), F32),
        compiler_params=_cparams(("arbitrary", "arbitrary")),
        name="out_proj_residual",
    )(o, w_bf16, x, gate)


def _router_kernel(x_ref, g_ref, sc_ref, sh_ref, wr_ref, bias_ref, u_ref,
                   h_ref, slab_ref, te_ref, gate_ref, rank_ref, cnt_ref, run_ref, *, tm):
    i = pl.program_id(0)
    per_group = N_EXPERTS // N_GROUPS

    @pl.when(i == 0)
    def _():
        run_ref[...] = jnp.zeros_like(run_ref)

    h = _norm_mod(x_ref[...], g_ref[...], sc_ref[0], sh_ref[0])
    h_ref[...] = h.astype(BF16)
    _store_slabs(slab_ref, 0, tm, h)
    logits = lax.dot_general(wr_ref[0], h, (((1,), (1,)), ((), ())),
                             precision=lax.Precision.HIGHEST, preferred_element_type=F32)
    scores = jax.nn.sigmoid(logits)
    biased = scores + bias_ref[0]
    neg = -jnp.inf
    sub = lax.broadcasted_iota(I32, (per_group, tm), 0)

    def first_max(vals, idx, sentinel):
        m = jnp.max(vals, axis=0, keepdims=True)
        return m, jnp.min(jnp.where(vals == m, idx, sentinel), axis=0, keepdims=True)

    sc_g = [scores[g * per_group:(g + 1) * per_group] for g in range(N_GROUPS)]
    bi_g = [biased[g * per_group:(g + 1) * per_group] for g in range(N_GROUPS)]

    gscore = jnp.zeros((N_GROUPS, tm), F32)
    for g in range(N_GROUPS):
        m1, i1 = first_max(bi_g[g], sub, per_group)
        m2 = jnp.max(jnp.where(sub == i1, neg, bi_g[g]), axis=0, keepdims=True)
        gscore = jnp.where(sub == g, m1 + m2, gscore)

    keep = jnp.zeros((N_GROUPS, tm), F32)
    cur = gscore
    for _ in range(TOPK_GROUPS):
        _, gi = first_max(cur, sub, N_GROUPS)
        hit = sub == gi
        keep = jnp.where(hit, 1.0, keep)
        cur = jnp.where(hit, neg, cur)

    masked = []
    for g in range(N_GROUPS):
        kg = jnp.max(jnp.where(sub == g, keep, 0.0), axis=0, keepdims=True)
        masked.append(jnp.where(kg > 0.0, bi_g[g], neg))
    eidx = [sub + g * per_group for g in range(N_GROUPS)]
    sel = [jnp.zeros((per_group, tm), F32) for _ in range(N_GROUPS)]

    top_e, gates = [], []
    for _ in range(TOP_K):
        m = masked[0].max(axis=0, keepdims=True)
        for g in range(1, N_GROUPS):
            m = jnp.maximum(m, jnp.max(masked[g], axis=0, keepdims=True))
        ei = jnp.full((1, tm), N_EXPERTS, I32)
        for g in range(N_GROUPS):
            ei = jnp.minimum(ei, jnp.min(jnp.where(masked[g] == m, eidx[g], N_EXPERTS), axis=0, keepdims=True))
        gt = jnp.zeros((1, tm), F32)
        for g in range(N_GROUPS):
            hit = eidx[g] == ei
            gt = gt + jnp.sum(jnp.where(hit, sc_g[g], 0.0), axis=0, keepdims=True)
            masked[g] = jnp.where(hit, neg, masked[g])
            sel[g] = jnp.where(hit, 1.0, sel[g])
        top_e.append(ei)
        gates.append(gt)

    gsum = gates[0]
    for gt in gates[1:]:
        gsum = gsum + gt

    sel_all = jnp.concatenate(sel, axis=0)
    before = _dot(sel_all.astype(BF16), u_ref[...]) + run_ref[...]
    run_new = run_ref[...] + jnp.sum(sel_all, axis=1, keepdims=True)
    run_ref[...] = run_new
    cnt_ref[...] = jnp.broadcast_to(run_new, cnt_ref.shape).astype(I32)

    te_ref[...] = jnp.zeros_like(te_ref)
    gate_ref[...] = jnp.zeros_like(gate_ref)
    rank_ref[...] = jnp.zeros_like(rank_ref)
    for j in range(TOP_K):
        rk = jnp.zeros((1, tm), F32)
        for g in range(N_GROUPS):
            rk = rk + jnp.sum(jnp.where(eidx[g] == top_e[j], before[g * per_group:(g + 1) * per_group], 0.0),
                              axis=0, keepdims=True)
        te_ref[j:j + 1, :] = top_e[j]
        gate_ref[j:j + 1, :] = gates[j] / gsum * ROUTED_SCALE
        rank_ref[j:j + 1, :] = rk.astype(I32)


def _route(x2, g, sc, sh, w_router, bias, layer, seq):
    t, d = x2.shape
    tm = ROUTE_TM
    e = N_EXPERTS
    slab = d // LANES
    jj = lax.broadcasted_iota(I32, (tm, tm), 0)
    ss = lax.broadcasted_iota(I32, (tm, tm), 1)
    u = (jj < ss).astype(BF16)
    per_seq = seq // tm
    kern = functools.partial(_router_kernel, tm=tm)
    rows = SUBLANES
    depth = w_router.shape[0]
    h, h_slabs, te, gate, rank, cnt = pl.pallas_call(
        kern,
        grid=(t // tm,),
        in_specs=[
            pl.BlockSpec((tm, d), lambda i: (i, 0)),
            pl.BlockSpec((1, d), lambda i: (0, 0)),
            pl.BlockSpec((1, 1, d), lambda i: (i // per_seq, 0, 0)),
            pl.BlockSpec((1, 1, d), lambda i: (i // per_seq, 0, 0)),
            pl.BlockSpec((1, e, d), lambda i: (layer, 0, 0)),
            pl.BlockSpec((1, e, 1), lambda i: (layer, 0, 0)),
            pl.BlockSpec((tm, tm), lambda i: (0, 0)),
        ],
        out_specs=[
            pl.BlockSpec((tm, d), lambda i: (i, 0)),
            pl.BlockSpec((tm * slab, LANES), lambda i: (i, 0)),
            pl.BlockSpec((rows, tm), lambda i: (0, i)),
            pl.BlockSpec((rows, tm), lambda i: (0, i)),
            pl.BlockSpec((rows, tm), lambda i: (0, i)),
            pl.BlockSpec((e, LANES), lambda i: (0, 0)),
        ],
        out_shape=[
            jax.ShapeDtypeStruct((t, d), BF16),
            jax.ShapeDtypeStruct((t * slab, LANES), F32),
            jax.ShapeDtypeStruct((rows, t), I32),
            jax.ShapeDtypeStruct((rows, t), F32),
            jax.ShapeDtypeStruct((rows, t), I32),
            jax.ShapeDtypeStruct((e, LANES), I32),
        ],
        scratch_shapes=[pltpu.VMEM((e, 1), F32)],
        compiler_params=_cparams(("arbitrary",)),
        name="norm_router_topk",
    )(x2, g.reshape(1, d), sc, sh, w_router, bias.reshape(depth, e, 1), u)
    return h, h_slabs, te[:TOP_K], gate[:TOP_K], rank[:TOP_K], cnt[:, 0]


def _dispatch_kernel(slot_ref, pe_ref, nu_ref, h_ref, hb_ref, wg_ref, wu_ref, wd_ref, xs_hbm, sh_ref,
                     zbuf, sem, zsem, *, tm, blk, n_blocks, slab):
    i = pl.program_id(0)
    n_used = nu_ref[0]
    group = 8

    def zero_copy(block):
        start = pl.multiple_of(block * (blk * slab), blk * slab)
        return pltpu.make_async_copy(zbuf, xs_hbm.at[pl.ds(start, blk * slab)], zsem)

    def expert_has_rows(e):
        return pe_ref[e] > (pe_ref[e - 1] if e else 0)

    @pl.when(i == 0)
    def _():
        zbuf[...] = jnp.zeros_like(zbuf)
        for phase in ("start", "wait"):
            for e in range(N_EXPERTS):
                @pl.when(expert_has_rows(e))
                def _():
                    cp = zero_copy(pe_ref[e] // blk - 1)
                    cp.start() if phase == "start" else cp.wait()

            def tail(bk, carry):
                cp = zero_copy(bk)
                cp.start() if phase == "start" else cp.wait()
                return carry

            lax.fori_loop(n_used, n_blocks, tail, 0)

    base = i * (TOP_K * tm)
    for j in range(TOP_K):
        def body(g, carry):
            for r in range(group):
                dst = pl.multiple_of(slot_ref[base + j * tm + g * group + r] * slab, slab)
                src = pl.multiple_of(g * (group * slab), group * slab) + r * slab
                pltpu.make_async_copy(h_ref.at[pl.ds(src, slab)], xs_hbm.at[pl.ds(dst, slab)], sem).start(
                    priority=r % 2)
            return carry

        lax.fori_loop(0, tm // group, body, 0)

    hb = hb_ref[...]
    mid = _silu(_dot(hb, wg_ref[0])) * _dot(hb, wu_ref[0])
    sh_ref[...] = _dot(mid.astype(BF16), wd_ref[0]).astype(BF16)

    for j in range(TOP_K):
        pltpu.make_async_copy(h_ref, xs_hbm.at[pl.ds(0, tm * slab)], sem).wait()


def _dispatch(h_slabs, h, slots_tiled, pad_end, n_used, n_blocks, slab, ws_gate, ws_up, ws_down, layer):
    t, d = h.shape
    f = ws_gate.shape[-1]
    tm = DISP_TM
    blk = MOE_BLK
    kern = functools.partial(_dispatch_kernel, tm=tm, blk=blk, n_blocks=n_blocks, slab=slab)
    grid_spec = pltpu.PrefetchScalarGridSpec(
        num_scalar_prefetch=3,
        grid=(t // tm,),
        in_specs=[
            pl.BlockSpec((tm * slab, LANES), lambda i, sl, pe, nu: (i, 0)),
            pl.BlockSpec((tm, d), lambda i, sl, pe, nu: (i, 0)),
            pl.BlockSpec((1, d, f), lambda i, sl, pe, nu: (layer, 0, 0)),
            pl.BlockSpec((1, d, f), lambda i, sl, pe, nu: (layer, 0, 0)),
            pl.BlockSpec((1, f, d), lambda i, sl, pe, nu: (layer, 0, 0)),
        ],
        out_specs=[
            pl.BlockSpec(memory_space=pl.ANY),
            pl.BlockSpec((tm, d), lambda i, sl, pe, nu: (i, 0)),
        ],
        scratch_shapes=[
            pltpu.VMEM((blk * slab, LANES), F32),
            pltpu.SemaphoreType.DMA,
            pltpu.SemaphoreType.DMA,
        ],
    )
    return pl.pallas_call(
        kern,
        grid_spec=grid_spec,
        out_shape=[
            jax.ShapeDtypeStruct((n_blocks * blk * slab, LANES), F32),
            jax.ShapeDtypeStruct((t, d), BF16),
        ],
        compiler_params=_cparams(("arbitrary",)),
        name="dispatch_rows",
    )(slots_tiled, pad_end, n_used, h_slabs, h, ws_gate, ws_up, ws_down)


def _expert_kernel(be_ref, ne_ref, nu_ref, x_ref, wg_hbm, wu_hbm, wd_hbm, y_ref,
                   wg32, wu32, wd32, wgb, wub, wdb, sem, *, blk, slab, layer):
    i = pl.program_id(0)
    n_used = nu_ref[0]

    def weight_copies(e):
        return [pltpu.make_async_copy(src.at[layer, e], dst, sem)
                for src, dst in ((wg_hbm, wg32), (wu_hbm, wu32), (wd_hbm, wd32))]

    @pl.when(i == 0)
    def _():
        for cp in weight_copies(be_ref[0]):
            cp.start()

    @pl.when(i < n_used)
    def _():
        prev = be_ref[jnp.maximum(i - 1, 0)]

        @pl.when((i == 0) | (be_ref[i] != prev))
        def _():
            for cp in weight_copies(be_ref[i]):
                cp.wait()
            wgb[...] = wg32[...].astype(BF16)
            wub[...] = wu32[...].astype(BF16)
            wdb[...] = wd32[...].astype(BF16)
            nxt = ne_ref[i]

            @pl.when(nxt >= 0)
            def _():
                for cp in weight_copies(nxt):
                    cp.start()

        x = jnp.concatenate([c.astype(BF16) for c in _load_slabs(x_ref, 0, blk, slab)], axis=1)
        mid = _silu(_dot(x, wgb[...])) * _dot(x, wub[...])
        _store_slabs(y_ref, 0, blk, _dot(mid.astype(BF16), wdb[...]))

    @pl.when(i >= n_used)
    def _():
        y_ref[...] = jnp.zeros_like(y_ref)


def _experts(xs, w_gate, w_up, w_down, layer, block_e, next_expert, n_used, slab):
    d, f = w_gate.shape[-2:]
    blk = MOE_BLK
    n_blocks = xs.shape[0] // (blk * slab)
    kern = functools.partial(_expert_kernel, blk=blk, slab=slab, layer=layer)
    grid_spec = pltpu.PrefetchScalarGridSpec(
        num_scalar_prefetch=3,
        grid=(n_blocks,),
        in_specs=[
            pl.BlockSpec((blk * slab, LANES), lambda i, be, ne, nu: (jnp.minimum(i, nu[0] - 1), 0)),
            pl.BlockSpec(memory_space=pl.ANY),
            pl.BlockSpec(memory_space=pl.ANY),
            pl.BlockSpec(memory_space=pl.ANY),
        ],
        out_specs=pl.BlockSpec((blk * slab, LANES), lambda i, be, ne, nu: (i, 0)),
        scratch_shapes=[
            pltpu.VMEM((d, f), F32),
            pltpu.VMEM((d, f), F32),
            pltpu.VMEM((f, d), F32),
            pltpu.VMEM((d, f), BF16),
            pltpu.VMEM((d, f), BF16),
            pltpu.VMEM((f, d), BF16),
            pltpu.SemaphoreType.DMA,
        ],
    )
    return pl.pallas_call(
        kern,
        grid_spec=grid_spec,
        out_shape=jax.ShapeDtypeStruct(xs.shape, F32),
        compiler_params=_cparams(("arbitrary",)),
        name="routed_experts",
    )(block_e, next_expert, n_used, xs, w_gate, w_up, w_down)


def _combine_kernel(slot_ref, y_hbm, sh_ref, x_ref, g_ref, gw_ref, o_ref, ybuf, sem, *, tm, slab, pitch):
    i = pl.program_id(0)
    n = pl.num_programs(0)
    rows = TOP_K * tm
    group = 8

    def issue(tile, slot):
        base = tile * rows

        def body(g, carry):
            for r in range(group):
                src = pl.multiple_of(slot_ref[base + g * group + r] * slab, slab)
                dst = pl.multiple_of((slot * rows + g * group) * pitch, group * pitch) + r * pitch
                pltpu.make_async_copy(y_hbm.at[pl.ds(src, slab)], ybuf.at[pl.ds(dst, slab)], sem.at[slot]).start(
                    priority=r % 2)
            return carry

        lax.fori_loop(0, rows // group, body, 0)

    @pl.when(i == 0)
    def _():
        issue(0, 0)

    @pl.when(i + 1 < n)
    def _():
        issue(i + 1, (i + 1) % 2)

    slot = i % 2
    buf0 = pl.multiple_of(slot * (rows * pitch), rows * pitch)
    pltpu.make_async_copy(y_hbm.at[pl.ds(0, rows * slab)], ybuf.at[pl.ds(0, rows * slab)], sem.at[slot]).wait()
    gw = gw_ref[...]
    gwb = [jnp.broadcast_to(gw[:, j:j + 1], (tm, LANES)) for j in range(TOP_K)]
    chunks = [jnp.zeros((tm, LANES), F32) for _ in range(slab)]
    for j in range(TOP_K):
        for s, y in enumerate(_load_slabs(ybuf, buf0 + j * (tm * pitch), tm, slab, pitch)):
            chunks[s] = chunks[s] + gwb[j] * y
    routed = jnp.concatenate(chunks, axis=1)
    o_ref[...] = x_ref[...] + g_ref[0] * (sh_ref[...].astype(F32) + routed)


def _combine(slots_tiled, y_slots, shared, x2, gate, gate_w, seq, slab):
    t, d = x2.shape
    tm = COMB_TM
    per_seq = seq // tm
    pitch = slab + SUBLANES
    kern = functools.partial(_combine_kernel, tm=tm, slab=slab, pitch=pitch)
    grid_spec = pltpu.PrefetchScalarGridSpec(
        num_scalar_prefetch=1,
        grid=(t // tm,),
        in_specs=[
            pl.BlockSpec(memory_space=pl.ANY),
            pl.BlockSpec((tm, d), lambda i, sl: (i, 0)),
            pl.BlockSpec((tm, d), lambda i, sl: (i, 0)),
            pl.BlockSpec((1, 1, d), lambda i, sl: (i // per_seq, 0, 0)),
            pl.BlockSpec((tm, SUBLANES), lambda i, sl: (i, 0)),
        ],
        out_specs=pl.BlockSpec((tm, d), lambda i, sl: (i, 0)),
        scratch_shapes=[
            pltpu.VMEM((2 * TOP_K * tm * pitch, LANES), F32),
            pltpu.SemaphoreType.DMA((2,)),
        ],
    )
    return pl.pallas_call(
        kern,
        grid_spec=grid_spec,
        out_shape=jax.ShapeDtypeStruct((t, d), F32),
        compiler_params=_cparams(("arbitrary",)),
        name="gather_combine",
    )(slots_tiled, y_slots, shared, x2, gate, gate_w)


def _tile_slots(slot, tm):
    k, t = slot.shape
    return slot.reshape(k, t // tm, tm).transpose(1, 0, 2).reshape(-1)


def _moe_layer(x, g, sc, sh, gate, w_router, router_bias, w_gate, w_up, w_down,
               ws_gate_b, ws_up_b, ws_down_b, layer):
    b, s, d = x.shape
    t = b * s
    x2 = x.reshape(t, d)
    slab = d // LANES
    h, h_slabs, top_e, gates, rank, counts = _route(x2, g, sc, sh, w_router, router_bias, layer, s)

    blk = MOE_BLK
    n_blocks = -(-(t * TOP_K) // blk) + N_EXPERTS
    padded = (counts + blk - 1) // blk * blk
    pad_end = jnp.cumsum(padded).astype(I32)
    pad_start = pad_end - padded
    experts = jnp.arange(N_EXPERTS, dtype=I32)
    slot = jnp.sum(jnp.where(top_e[..., None] == experts, pad_start, 0), axis=-1) + rank
    block_first = jnp.arange(n_blocks, dtype=I32) * blk
    block_e = jnp.minimum(jnp.sum((pad_end[None, :] <= block_first[:, None]).astype(I32), axis=1), N_EXPERTS - 1)
    n_used = pad_end[-1:] // blk
    has_rows = padded > 0
    later = jnp.where(has_rows[None, :] & (experts[None, :] > experts[:, None]), experts[None, :], N_EXPERTS)
    following = jnp.min(later, axis=1)
    following = jnp.where(following < N_EXPERTS, following, -1)
    block_hot = block_e[:, None] == experts[None, :]
    next_expert = jnp.sum(jnp.where(block_hot, following[None, :], 0), axis=1).astype(I32)

    xs, shared = _dispatch(h_slabs, h, _tile_slots(slot, DISP_TM), pad_end, n_used, n_blocks, slab,
                           ws_gate_b, ws_up_b, ws_down_b, layer)
    y_slots = _experts(xs, w_gate, w_up, w_down, layer, block_e, next_expert, n_used, slab)
    gate_w = jnp.pad(gates.T, ((0, 0), (0, SUBLANES - TOP_K)))
    out = _combine(_tile_slots(slot, COMB_TM), y_slots, shared, x2, gate, gate_w, s, slab)
    return out.reshape(b, s, d)


def kernel(x, c, w_mod, b_mod, norm_mix, norm_ffn, w_in, w_out, q_norm, k_norm, lambda_q1, lambda_k1,
           lambda_q2, lambda_k2, subln, w_router, router_bias, w_gate, w_up, w_down, ws_gate, ws_up, ws_down):
    b, s, d = x.shape
    depth = w_mod.shape[0]
    assert d == SB_HEADS * HEAD_DIM == 2 * DA_HEADS * HEAD_DIM
    assert s % max(SB_TQ, SB_TK, DA_TQ, DA_TK, PROJ_TM, ROUTE_TM, DISP_TM) == 0
    assert SB_TQ % SB_TK == 0 and DA_TK % DA_TQ == 0
    q_scale = HEAD_DIM ** -0.5 * LOG2E

    mod = _modulation(c, w_mod, b_mod)
    ws_gate_b, ws_up_b, ws_down_b = ws_gate.astype(BF16), ws_up.astype(BF16), ws_down.astype(BF16)
    for i in range(depth):
        sh1, sc1, g1, sh2, sc2, g2 = [m.reshape(b, 1, d) for m in jnp.split(mod[i], 6, axis=-1)]
        w_in_b = w_in[i].astype(BF16)
        if i % 2 == 0:
            colscale = jnp.stack([jnp.full((1, d), q_scale, F32), jnp.ones((1, d), F32), jnp.ones((1, d), F32)])
            qk, v = _qkv_proj(x, norm_mix[i], sc1, sh1, w_in_b, colscale, qk_norm=False, v_width=HEAD_DIM)
            o = _sb_attention(qk, v, d)
        else:
            j = i // 2
            lambda_init = 0.8 - 0.6 * math.exp(-0.3 * i)
            reps = d // HEAD_DIM
            colscale = jnp.stack([
                jnp.tile(q_norm[j].astype(F32), reps)[None, :] * q_scale,
                jnp.tile(k_norm[j].astype(F32), reps)[None, :],
                jnp.ones((1, d), F32)])
            qk, v = _qkv_proj(x, norm_mix[i], sc1, sh1, w_in_b, colscale, qk_norm=True, v_width=2 * HEAD_DIM)
            lam_rows = jnp.stack([lambda_q1[j], lambda_k1[j], lambda_q2[j], lambda_k2[j]]).astype(F32)
            o = _da_attention(qk, v, lam_rows, subln[j].astype(F32), d, lambda_init)
        x = _out_proj(o, w_out[i].astype(BF16), x, g1)
        x = _moe_layer(x, norm_ffn[i], sc2, sh2, g2, w_router, router_bias, w_gate, w_up, w_down,
                       ws_gate_b, ws_up_b, ws_down_b, i)
    return x
```

```python
import functools
import math

import jax
import jax.numpy as jnp
from jax import lax
from jax.experimental import pallas as pl
from jax.experimental.pallas import tpu as pltpu

F32 = jnp.float32
BF16 = jnp.bfloat16
I32 = jnp.int32

LANES = 128
SUBLANES = 8
VMEM_LIMIT = 56 * 1024 * 1024

SB_HEADS = 16
DA_HEADS = 8
HEAD_DIM = 128
N_EXPERTS = 64
TOP_K = 6
N_GROUPS = 8
TOPK_GROUPS = 4
ROUTED_SCALE = 2.5
EPS = 1e-6
SUBLN_EPS = 1e-5
LOG2E = math.log2(math.e)

SB_TQ, SB_TK = 512, 256
DA_TQ, DA_TK = 512, 1024
PROJ_TM = 512
ROUTE_TM = 512
MOE_BLK = 256
DISP_TM = 512
COMB_TM = 128


def _cparams(sem):
    return pltpu.CompilerParams(dimension_semantics=sem, vmem_limit_bytes=VMEM_LIMIT)


def _silu(x):
    return x * jax.nn.sigmoid(x)


def _dot(a, b):
    return jnp.dot(a, b, preferred_element_type=F32)


def _dot_nt(a, b):
    return lax.dot_general(a, b, (((1,), (1,)), ((), ())), preferred_element_type=F32)


def _store_slabs(ref, base, rows, vals):
    slab = vals.shape[1] // LANES
    for s in range(slab):
        ref[pl.ds(base + s, rows, stride=slab), :] = vals[:, s * LANES:(s + 1) * LANES]


def _load_slabs(ref, base, rows, slab, pitch=None):
    return [ref[pl.ds(base + s, rows, stride=pitch or slab), :] for s in range(slab)]


def _mod_kernel(c_ref, w_ref, b_ref, o_ref):
    cond = _silu(c_ref[...])
    o_ref[0] = _dot(cond.astype(BF16), w_ref[0].astype(BF16)) + b_ref[0]


def _modulation(c, w_mod, b_mod):
    depth, d, n = w_mod.shape
    b = c.shape[0]
    rows = -(-b // SUBLANES) * SUBLANES
    cp = jnp.pad(c, ((0, rows - b), (0, 0)))
    tn = 1024
    out = pl.pallas_call(
        _mod_kernel,
        grid=(depth, n // tn),
        in_specs=[
            pl.BlockSpec((rows, d), lambda l, j: (0, 0)),
            pl.BlockSpec((1, d, tn), lambda l, j: (l, 0, j)),
            pl.BlockSpec((1, 1, tn), lambda l, j: (l, 0, j)),
        ],
        out_specs=pl.BlockSpec((1, rows, tn), lambda l, j: (l, 0, j)),
        out_shape=jax.ShapeDtypeStruct((depth, rows, n), F32),
        compiler_params=_cparams(("arbitrary", "arbitrary")),
        name="adaln_mod",
    )(cp, w_mod, b_mod.reshape(depth, 1, n))
    return out[:, :b]


def _norm_mod(x, g, sc, sh):
    ms = jnp.mean(x * x, axis=-1, keepdims=True)
    return x * lax.rsqrt(ms + EPS) * g * (1.0 + sc) + sh


def _qkv_kernel(x_ref, g_ref, sc_ref, sh_ref, w_ref, cs_ref, qk_ref, v_ref, h_ref, *, qk_norm, v_width):
    j = pl.program_id(2)

    @pl.when(j == 0)
    def _():
        h_ref[...] = _norm_mod(x_ref[0], g_ref[...], sc_ref[0], sh_ref[0]).astype(BF16)

    acc = _dot(h_ref[...], w_ref[...])
    n_chunks = acc.shape[1] // HEAD_DIM

    @pl.when(j < 2)
    def _():
        for c in range(n_chunks):
            a = acc[:, c * HEAD_DIM:(c + 1) * HEAD_DIM]
            if qk_norm:
                a = a * lax.rsqrt(jnp.mean(a * a, axis=-1, keepdims=True) + EPS)
            a = a * cs_ref[0][:, c * HEAD_DIM:(c + 1) * HEAD_DIM]
            qk_ref[0, c] = a.astype(BF16)

    @pl.when(j == 2)
    def _():
        for c in range(acc.shape[1] // v_width):
            v_ref[0, c] = acc[:, c * v_width:(c + 1) * v_width].astype(BF16)


def _qkv_proj(x, g, sc, sh, w_bf16, colscale, *, qk_norm, v_width):
    b, s, d = x.shape
    tm = PROJ_TM
    n_ch = d // HEAD_DIM
    kern = functools.partial(_qkv_kernel, qk_norm=qk_norm, v_width=v_width)
    return pl.pallas_call(
        kern,
        grid=(b, s // tm, 3),
        in_specs=[
            pl.BlockSpec((1, tm, d), lambda bi, i, j: (bi, i, 0)),
            pl.BlockSpec((1, d), lambda bi, i, j: (0, 0)),
            pl.BlockSpec((1, 1, d), lambda bi, i, j: (bi, 0, 0)),
            pl.BlockSpec((1, 1, d), lambda bi, i, j: (bi, 0, 0)),
            pl.BlockSpec((d, d), lambda bi, i, j: (0, j)),
            pl.BlockSpec((1, 1, d), lambda bi, i, j: (j, 0, 0)),
        ],
        out_specs=[
            pl.BlockSpec((1, n_ch, tm, HEAD_DIM), lambda bi, i, j: (bi, jnp.minimum(j, 1), i, 0)),
            pl.BlockSpec((1, d // v_width, tm, v_width), lambda bi, i, j: (bi, 0, i, 0)),
        ],
        out_shape=[
            jax.ShapeDtypeStruct((b, 2 * n_ch, s, HEAD_DIM), BF16),
            jax.ShapeDtypeStruct((b, d // v_width, s, v_width), BF16),
        ],
        scratch_shapes=[pltpu.VMEM((tm, d), BF16)],
        compiler_params=_cparams(("arbitrary", "arbitrary", "arbitrary")),
        name="norm_qkv_proj",
    )(x, g.reshape(1, d), sc, sh, w_bf16, colscale)


def _sb_kernel(q_ref, k_ref, v_ref, u_ref, o_ref, *, tq, tk):
    i = pl.program_id(2)
    q = q_ref[0, 0]
    per = tq // tk
    rows = i * tq + lax.broadcasted_iota(I32, (tq, 1), 0)
    u = u_ref[...]


    def weights(c, r0, run, masked):
        start = pl.multiple_of(c * tk, tk)
        z = _dot_nt(q[r0:], k_ref[0, 0, pl.ds(start, tk), :])
        softplus = jnp.maximum(z, 0.0) + jnp.log(1.0 + jnp.exp2(-jnp.abs(z))) * LOG2E
        if masked:
            earlier = (start + lax.broadcasted_iota(I32, (1, tk), 1)) < rows[r0:]
            softplus = jnp.where(earlier, softplus, 0.0)
        between = _dot(softplus.astype(BF16), u) + run[r0:]
        a = jnp.exp2((z - softplus) + between)
        if masked:
            a = jnp.where(earlier, a, 0.0)
        dec = jnp.sum(softplus, axis=-1, keepdims=True)
        run = jnp.concatenate([run[:r0], run[r0:] - dec], axis=0) if r0 else run - dec
        return a.astype(BF16), run

    def values(c, r0, a, acc):
        pv = _dot(a, v_ref[0, 0, pl.ds(pl.multiple_of(c * tk, tk), tk), :])
        return jnp.concatenate([acc[:r0], acc[r0:] + pv], axis=0) if r0 else acc + pv

    def group(chunks, run, acc, masked):
        ws = []
        for c, r0 in chunks:
            a, run = weights(c, r0, run, masked)
            ws.append(a)
        for (c, r0), a in zip(chunks, ws):
            acc = values(c, r0, a, acc)
        return run, acc

    run, acc = group([(per * i + m, m * tk) for m in reversed(range(per))],
                     jnp.zeros((tq, 1), F32), jnp.zeros((tq, HEAD_DIM), F32), True)
    def sweep(top, n, carry):
        return group([(top - r, 0) for r in range(n)], carry[0], carry[1], False)

    odd = i % 2
    top = per * i - 1
    carry = lax.fori_loop(0, odd, lambda _, cr: sweep(top, per, cr), (run, acc))
    top = top - per * odd
    _, acc = lax.fori_loop(0, i // 2, lambda b, cr: sweep(top - b * (2 * per), 2 * per, cr), carry)
    o_ref[0] = acc.astype(BF16)


def _sb_attention(qk, v, d):
    b, _, s, _ = qk.shape
    tq, tk = SB_TQ, SB_TK
    heads = SB_HEADS
    jj = lax.broadcasted_iota(I32, (tk, tk), 0)
    ss = lax.broadcasted_iota(I32, (tk, tk), 1)
    u = jnp.where(jj > ss, -1.0, 0.0).astype(BF16)
    kern = functools.partial(_sb_kernel, tq=tq, tk=tk)
    return pl.pallas_call(
        kern,
        grid=(b, heads, s // tq),
        in_specs=[
            pl.BlockSpec((1, 1, tq, HEAD_DIM), lambda bi, h, i: (bi, h, i, 0)),
            pl.BlockSpec((1, 1, s, HEAD_DIM), lambda bi, h, i: (bi, heads + h, 0, 0)),
            pl.BlockSpec((1, 1, s, HEAD_DIM), lambda bi, h, i: (bi, h, 0, 0)),
            pl.BlockSpec((tk, tk), lambda bi, h, i: (0, 0)),
        ],
        out_specs=pl.BlockSpec((1, tq, HEAD_DIM), lambda bi, h, i: (bi, i, h)),
        out_shape=jax.ShapeDtypeStruct((b, s, d), BF16),
        compiler_params=_cparams(("arbitrary", "arbitrary", "arbitrary")),
        name="stickbreak_attn",
    )(qk, qk, v, u)


def _da_kernel(q_ref, k_ref, v_ref, slope_ref, lam_ref, subln_ref, o_ref, *, tq, tk, lambda_init):
    i = pl.program_id(2)
    row0 = i * tq
    rows = row0 + lax.broadcasted_iota(I32, (tq, 1), 0)
    slope = slope_ref[0]
    qs = (q_ref[0, 0], q_ref[0, 1])
    dv = v_ref.shape[-1]

    def chunk(start, width, carry, masked):
        start = pl.multiple_of(start, tq)
        cols = start + lax.broadcasted_iota(I32, (1, width), 1)
        key_bias = slope * (cols - row0).astype(F32)
        v = v_ref[0, 0, pl.ds(start, width), :]
        new = []
        for m in range(2):
            mx, den, acc = carry[3 * m:3 * m + 3]
            k = k_ref[0, m, pl.ds(start, width), :]
            z = _dot_nt(qs[m], k) + key_bias
            if masked:
                z = jnp.where(cols <= rows, z, -jnp.inf)
            mx_new = jnp.maximum(mx, jnp.max(z, axis=-1, keepdims=True))
            alpha = jnp.exp2(mx - mx_new)
            p = jnp.exp2(z - mx_new)
            den = alpha * den + jnp.sum(p, axis=-1, keepdims=True)
            acc = alpha * acc + _dot(p.astype(BF16), v)
            new += [mx_new, den, acc]
        return tuple(new)

    init = []
    for _ in range(2):
        init += [jnp.full((tq, 1), -jnp.inf, F32), jnp.zeros((tq, 1), F32), jnp.zeros((tq, dv), F32)]
    n_below = row0 // tk
    carry = lax.fori_loop(0, n_below, lambda c, cr: chunk(c * tk, tk, cr, False), tuple(init))
    carry = chunk(n_below * tk, tk, carry, True)
    _, l1, a1, _, l2, a2 = carry

    lv = lam_ref[...]
    s1 = jnp.sum(lv[0:1] * lv[1:2], axis=-1, keepdims=True)
    s2 = jnp.sum(lv[2:3] * lv[3:4], axis=-1, keepdims=True)
    lam = jnp.exp(s1) - jnp.exp(s2) + lambda_init
    o = a1 / l1 - lam * (a2 / l2)
    o = o * lax.rsqrt(jnp.mean(o * o, axis=-1, keepdims=True) + SUBLN_EPS) * subln_ref[...]
    o_ref[0] = (o * (1.0 - lambda_init)).astype(BF16)


def _da_attention(qk, v, lam_rows, subln, d, lambda_init):
    b, _, s, _ = qk.shape
    tq, tk = DA_TQ, DA_TK
    heads = DA_HEADS
    dv = v.shape[-1]
    slopes = jnp.asarray(
        [2.0 ** (-8.0 * (h + 1) / heads) * LOG2E for h in range(heads)], F32).reshape(heads, 1, 1)
    kern = functools.partial(_da_kernel, tq=tq, tk=tk, lambda_init=lambda_init)
    return pl.pallas_call(
        kern,
        grid=(b, heads, s // tq),
        in_specs=[
            pl.BlockSpec((1, 2, tq, HEAD_DIM), lambda bi, h, i: (bi, h, i, 0)),
            pl.BlockSpec((1, 2, s, HEAD_DIM), lambda bi, h, i: (bi, heads + h, 0, 0)),
            pl.BlockSpec((1, 1, s, dv), lambda bi, h, i: (bi, h, 0, 0)),
            pl.BlockSpec((1, 1, 1), lambda bi, h, i: (h, 0, 0)),
            pl.BlockSpec((4, HEAD_DIM), lambda bi, h, i: (0, 0)),
            pl.BlockSpec((1, dv), lambda bi, h, i: (0, 0)),
        ],
        out_specs=pl.BlockSpec((1, tq, dv), lambda bi, h, i: (bi, i, h)),
        out_shape=jax.ShapeDtypeStruct((b, s, d), BF16),
        compiler_params=_cparams(("arbitrary", "arbitrary", "arbitrary")),
        name="diff_attn",
    )(qk, qk, v, slopes, lam_rows, subln.reshape(1, dv))


def _outproj_kernel(o_ref, w_ref, x_ref, g_ref, y_ref):
    y_ref[0] = x_ref[0] + g_ref[0] * _dot(o_ref[0], w_ref[...])


def _out_proj(o, w_bf16, x, gate):
    b, s, d = x.shape
    tm = PROJ_TM
    return pl.pallas_call(
        _outproj_kernel,
        grid=(b, s // tm),
        in_specs=[
            pl.BlockSpec((1, tm, d), lambda bi, i: (bi, i, 0)),
            pl.BlockSpec((d, d), lambda bi, i: (0, 0)),
            pl.BlockSpec((1, tm, d), lambda bi, i: (bi, i, 0)),
            pl.BlockSpec((1, 1, d), lambda bi, i: (bi, 0, 0)),
        ],
        out_specs=pl.BlockSpec((1, tm, d), lambda bi, i: (bi, i, 0)),
        out_shape=jax.ShapeDtypeStruct((b, s, d), F32),
        compiler_params=_cparams(("arbitrary", "arbitrary")),
        name="out_proj_residual",
    )(o, w_bf16, x, gate)


def _router_kernel(x_ref, g_ref, sc_ref, sh_ref, wr_ref, bias_ref, u_ref,
                   h_ref, slab_ref, te_ref, gate_ref, rank_ref, cnt_ref, run_ref, *, tm):
    i = pl.program_id(0)
    per_group = N_EXPERTS // N_GROUPS

    @pl.when(i == 0)
    def _():
        run_ref[...] = jnp.zeros_like(run_ref)

    h = _norm_mod(x_ref[...], g_ref[...], sc_ref[0], sh_ref[0])
    h_ref[...] = h.astype(BF16)
    _store_slabs(slab_ref, 0, tm, h)
    logits = lax.dot_general(wr_ref[0], h, (((1,), (1,)), ((), ())),
                             precision=lax.Precision.HIGHEST, preferred_element_type=F32)
    scores = jax.nn.sigmoid(logits)
    biased = scores + bias_ref[0]
    neg = -jnp.inf
    sub = lax.broadcasted_iota(I32, (per_group, tm), 0)

    def first_max(vals, idx, sentinel):
        m = jnp.max(vals, axis=0, keepdims=True)
        return m, jnp.min(jnp.where(vals == m, idx, sentinel), axis=0, keepdims=True)

    sc_g = [scores[g * per_group:(g + 1) * per_group] for g in range(N_GROUPS)]
    bi_g = [biased[g * per_group:(g + 1) * per_group] for g in range(N_GROUPS)]

    gscore = jnp.zeros((N_GROUPS, tm), F32)
    for g in range(N_GROUPS):
        m1, i1 = first_max(bi_g[g], sub, per_group)
        m2 = jnp.max(jnp.where(sub == i1, neg, bi_g[g]), axis=0, keepdims=True)
        gscore = jnp.where(sub == g, m1 + m2, gscore)

    keep = jnp.zeros((N_GROUPS, tm), F32)
    cur = gscore
    for _ in range(TOPK_GROUPS):
        _, gi = first_max(cur, sub, N_GROUPS)
        hit = sub == gi
        keep = jnp.where(hit, 1.0, keep)
        cur = jnp.where(hit, neg, cur)

    masked = []
    for g in range(N_GROUPS):
        kg = jnp.max(jnp.where(sub == g, keep, 0.0), axis=0, keepdims=True)
        masked.append(jnp.where(kg > 0.0, bi_g[g], neg))
    eidx = [sub + g * per_group for g in range(N_GROUPS)]
    sel = [jnp.zeros((per_group, tm), F32) for _ in range(N_GROUPS)]

    top_e, gates = [], []
    for _ in range(TOP_K):
        m = masked[0].max(axis=0, keepdims=True)
        for g in range(1, N_GROUPS):
            m = jnp.maximum(m, jnp.max(masked[g], axis=0, keepdims=True))
        ei = jnp.full((1, tm), N_EXPERTS, I32)
        for g in range(N_GROUPS):
            ei = jnp.minimum(ei, jnp.min(jnp.where(masked[g] == m, eidx[g], N_EXPERTS), axis=0, keepdims=True))
        gt = jnp.zeros((1, tm), F32)
        for g in range(N_GROUPS):
            hit = eidx[g] == ei
            gt = gt + jnp.sum(jnp.where(hit, sc_g[g], 0.0), axis=0, keepdims=True)
            masked[g] = jnp.where(hit, neg, masked[g])
            sel[g] = jnp.where(hit, 1.0, sel[g])
        top_e.append(ei)
        gates.append(gt)

    gsum = gates[0]
    for gt in gates[1:]:
        gsum = gsum + gt

    sel_all = jnp.concatenate(sel, axis=0)
    before = _dot(sel_all.astype(BF16), u_ref[...]) + run_ref[...]
    run_new = run_ref[...] + jnp.sum(sel_all, axis=1, keepdims=True)
    run_ref[...] = run_new
    cnt_ref[...] = jnp.broadcast_to(run_new, cnt_ref.shape).astype(I32)

    te_ref[...] = jnp.zeros_like(te_ref)
    gate_ref[...] = jnp.zeros_like(gate_ref)
    rank_ref[...] = jnp.zeros_like(rank_ref)
    for j in range(TOP_K):
        rk = jnp.zeros((1, tm), F32)
        for g in range(N_GROUPS):
            rk = rk + jnp.sum(jnp.where(eidx[g] == top_e[j], before[g * per_group:(g + 1) * per_group], 0.0),
                              axis=0, keepdims=True)
        te_ref[j:j + 1, :] = top_e[j]
        gate_ref[j:j + 1, :] = gates[j] / gsum * ROUTED_SCALE
        rank_ref[j:j + 1, :] = rk.astype(I32)


def _route(x2, g, sc, sh, w_router, bias, layer, seq):
    t, d = x2.shape
    tm = ROUTE_TM
    e = N_EXPERTS
    slab = d // LANES
    jj = lax.broadcasted_iota(I32, (tm, tm), 0)
    ss = lax.broadcasted_iota(I32, (tm, tm), 1)
    u = (jj < ss).astype(BF16)
    per_seq = seq // tm
    kern = functools.partial(_router_kernel, tm=tm)
    rows = SUBLANES
    depth = w_router.shape[0]
    h, h_slabs, te, gate, rank, cnt = pl.pallas_call(
        kern,
        grid=(t // tm,),
        in_specs=[
            pl.BlockSpec((tm, d), lambda i: (i, 0)),
            pl.BlockSpec((1, d), lambda i: (0, 0)),
            pl.BlockSpec((1, 1, d), lambda i: (i // per_seq, 0, 0)),
            pl.BlockSpec((1, 1, d), lambda i: (i // per_seq, 0, 0)),
            pl.BlockSpec((1, e, d), lambda i: (layer, 0, 0)),
            pl.BlockSpec((1, e, 1), lambda i: (layer, 0, 0)),
            pl.BlockSpec((tm, tm), lambda i: (0, 0)),
        ],
        out_specs=[
            pl.BlockSpec((tm, d), lambda i: (i, 0)),
            pl.BlockSpec((tm * slab, LANES), lambda i: (i, 0)),
            pl.BlockSpec((rows, tm), lambda i: (0, i)),
            pl.BlockSpec((rows, tm), lambda i: (0, i)),
            pl.BlockSpec((rows, tm), lambda i: (0, i)),
            pl.BlockSpec((e, LANES), lambda i: (0, 0)),
        ],
        out_shape=[
            jax.ShapeDtypeStruct((t, d), BF16),
            jax.ShapeDtypeStruct((t * slab, LANES), F32),
            jax.ShapeDtypeStruct((rows, t), I32),
            jax.ShapeDtypeStruct((rows, t), F32),
            jax.ShapeDtypeStruct((rows, t), I32),
            jax.ShapeDtypeStruct((e, LANES), I32),
        ],
        scratch_shapes=[pltpu.VMEM((e, 1), F32)],
        compiler_params=_cparams(("arbitrary",)),
        name="norm_router_topk",
    )(x2, g.reshape(1, d), sc, sh, w_router, bias.reshape(depth, e, 1), u)
    return h, h_slabs, te[:TOP_K], gate[:TOP_K], rank[:TOP_K], cnt[:, 0]


def _dispatch_kernel(slot_ref, pe_ref, nu_ref, h_ref, hb_ref, wg_ref, wu_ref, wd_ref, xs_hbm, sh_ref,
                     zbuf, sem, zsem, *, tm, blk, n_blocks, slab):
    i = pl.program_id(0)
    n_used = nu_ref[0]
    group = 8

    def zero_copy(block):
        start = pl.multiple_of(block * (blk * slab), blk * slab)
        return pltpu.make_async_copy(zbuf, xs_hbm.at[pl.ds(start, blk * slab)], zsem)

    def expert_has_rows(e):
        return pe_ref[e] > (pe_ref[e - 1] if e else 0)

    @pl.when(i == 0)
    def _():
        zbuf[...] = jnp.zeros_like(zbuf)
        for phase in ("start", "wait"):
            for e in range(N_EXPERTS):
                @pl.when(expert_has_rows(e))
                def _():
                    cp = zero_copy(pe_ref[e] // blk - 1)
                    cp.start() if phase == "start" else cp.wait()

            def tail(bk, carry):
                cp = zero_copy(bk)
                cp.start() if phase == "start" else cp.wait()
                return carry

            lax.fori_loop(n_used, n_blocks, tail, 0)

    base = i * (TOP_K * tm)
    for j in range(TOP_K):
        def body(g, carry):
            for r in range(group):
                dst = pl.multiple_of(slot_ref[base + j * tm + g * group + r] * slab, slab)
                src = pl.multiple_of(g * (group * slab), group * slab) + r * slab
                pltpu.make_async_copy(h_ref.at[pl.ds(src, slab)], xs_hbm.at[pl.ds(dst, slab)], sem).start(
                    priority=r % 2)
            return carry

        lax.fori_loop(0, tm // group, body, 0)

    hb = hb_ref[...]
    mid = _silu(_dot(hb, wg_ref[0])) * _dot(hb, wu_ref[0])
    sh_ref[...] = _dot(mid.astype(BF16), wd_ref[0]).astype(BF16)

    for j in range(TOP_K):
        pltpu.make_async_copy(h_ref, xs_hbm.at[pl.ds(0, tm * slab)], sem).wait()


def _dispatch(h_slabs, h, slots_tiled, pad_end, n_used, n_blocks, slab, ws_gate, ws_up, ws_down, layer):
    t, d = h.shape
    f = ws_gate.shape[-1]
    tm = DISP_TM
    blk = MOE_BLK
    kern = functools.partial(_dispatch_kernel, tm=tm, blk=blk, n_blocks=n_blocks, slab=slab)
    grid_spec = pltpu.PrefetchScalarGridSpec(
        num_scalar_prefetch=3,
        grid=(t // tm,),
        in_specs=[
            pl.BlockSpec((tm * slab, LANES), lambda i, sl, pe, nu: (i, 0)),
            pl.BlockSpec((tm, d), lambda i, sl, pe, nu: (i, 0)),
            pl.BlockSpec((1, d, f), lambda i, sl, pe, nu: (layer, 0, 0)),
            pl.BlockSpec((1, d, f), lambda i, sl, pe, nu: (layer, 0, 0)),
            pl.BlockSpec((1, f, d), lambda i, sl, pe, nu: (layer, 0, 0)),
        ],
        out_specs=[
            pl.BlockSpec(memory_space=pl.ANY),
            pl.BlockSpec((tm, d), lambda i, sl, pe, nu: (i, 0)),
        ],
        scratch_shapes=[
            pltpu.VMEM((blk * slab, LANES), F32),
            pltpu.SemaphoreType.DMA,
            pltpu.SemaphoreType.DMA,
        ],
    )
    return pl.pallas_call(
        kern,
        grid_spec=grid_spec,
        out_shape=[
            jax.ShapeDtypeStruct((n_blocks * blk * slab, LANES), F32),
            jax.ShapeDtypeStruct((t, d), BF16),
        ],
        compiler_params=_cparams(("arbitrary",)),
        name="dispatch_rows",
    )(slots_tiled, pad_end, n_used, h_slabs, h, ws_gate, ws_up, ws_down)


def _expert_kernel(be_ref, ne_ref, nu_ref, x_ref, wg_hbm, wu_hbm, wd_hbm, y_ref,
                   wg32, wu32, wd32, wgb, wub, wdb, sem, *, blk, slab, layer):
    i = pl.program_id(0)
    n_used = nu_ref[0]

    def weight_copies(e):
        return [pltpu.make_async_copy(src.at[layer, e], dst, sem)
                for src, dst in ((wg_hbm, wg32), (wu_hbm, wu32), (wd_hbm, wd32))]

    @pl.when(i == 0)
    def _():
        for cp in weight_copies(be_ref[0]):
            cp.start()

    @pl.when(i < n_used)
    def _():
        prev = be_ref[jnp.maximum(i - 1, 0)]

        @pl.when((i == 0) | (be_ref[i] != prev))
        def _():
            for cp in weight_copies(be_ref[i]):
                cp.wait()
            wgb[...] = wg32[...].astype(BF16)
            wub[...] = wu32[...].astype(BF16)
            wdb[...] = wd32[...].astype(BF16)
            nxt = ne_ref[i]

            @pl.when(nxt >= 0)
            def _():
                for cp in weight_copies(nxt):
                    cp.start()

        x = jnp.concatenate([c.astype(BF16) for c in _load_slabs(x_ref, 0, blk, slab)], axis=1)
        mid = _silu(_dot(x, wgb[...])) * _dot(x, wub[...])
        _store_slabs(y_ref, 0, blk, _dot(mid.astype(BF16), wdb[...]))

    @pl.when(i >= n_used)
    def _():
        y_ref[...] = jnp.zeros_like(y_ref)


def _experts(xs, w_gate, w_up, w_down, layer, block_e, next_expert, n_used, slab):
    d, f = w_gate.shape[-2:]
    blk = MOE_BLK
    n_blocks = xs.shape[0] // (blk * slab)
    kern = functools.partial(_expert_kernel, blk=blk, slab=slab, layer=layer)
    grid_spec = pltpu.PrefetchScalarGridSpec(
        num_scalar_prefetch=3,
        grid=(n_blocks,),
        in_specs=[
            pl.BlockSpec((blk * slab, LANES), lambda i, be, ne, nu: (jnp.minimum(i, nu[0] - 1), 0)),
            pl.BlockSpec(memory_space=pl.ANY),
            pl.BlockSpec(memory_space=pl.ANY),
            pl.BlockSpec(memory_space=pl.ANY),
        ],
        out_specs=pl.BlockSpec((blk * slab, LANES), lambda i, be, ne, nu: (i, 0)),
        scratch_shapes=[
            pltpu.VMEM((d, f), F32),
            pltpu.VMEM((d, f), F32),
            pltpu.VMEM((f, d), F32),
            pltpu.VMEM((d, f), BF16),
            pltpu.VMEM((d, f), BF16),
            pltpu.VMEM((f, d), BF16),
            pltpu.SemaphoreType.DMA,
        ],
    )
    return pl.pallas_call(
        kern,
        grid_spec=grid_spec,
        out_shape=jax.ShapeDtypeStruct(xs.shape, F32),
        compiler_params=_cparams(("arbitrary",)),
        name="routed_experts",
    )(block_e, next_expert, n_used, xs, w_gate, w_up, w_down)


def _combine_kernel(slot_ref, y_hbm, sh_ref, x_ref, g_ref, gw_ref, o_ref, ybuf, sem, *, tm, slab, pitch):
    i = pl.program_id(0)
    n = pl.num_programs(0)
    rows = TOP_K * tm
    group = 8

    def issue(tile, slot):
        base = tile * rows

        def body(g, carry):
            for r in range(group):
                src = pl.multiple_of(slot_ref[base + g * group + r] * slab, slab)
                dst = pl.multiple_of((slot * rows + g * group) * pitch, group * pitch) + r * pitch
                pltpu.make_async_copy(y_hbm.at[pl.ds(src, slab)], ybuf.at[pl.ds(dst, slab)], sem.at[slot]).start(
                    priority=r % 2)
            return carry

        lax.fori_loop(0, rows // group, body, 0)

    @pl.when(i == 0)
    def _():
        issue(0, 0)

    @pl.when(i + 1 < n)
    def _():
        issue(i + 1, (i + 1) % 2)

    slot = i % 2
    buf0 = pl.multiple_of(slot * (rows * pitch), rows * pitch)
    pltpu.make_async_copy(y_hbm.at[pl.ds(0, rows * slab)], ybuf.at[pl.ds(0, rows * slab)], sem.at[slot]).wait()
    gw = gw_ref[...]
    gwb = [jnp.broadcast_to(gw[:, j:j + 1], (tm, LANES)) for j in range(TOP_K)]
    chunks = [jnp.zeros((tm, LANES), F32) for _ in range(slab)]
    for j in range(TOP_K):
        for s, y in enumerate(_load_slabs(ybuf, buf0 + j * (tm * pitch), tm, slab, pitch)):
            chunks[s] = chunks[s] + gwb[j] * y
    routed = jnp.concatenate(chunks, axis=1)
    o_ref[...] = x_ref[...] + g_ref[0] * (sh_ref[...].astype(F32) + routed)


def _combine(slots_tiled, y_slots, shared, x2, gate, gate_w, seq, slab):
    t, d = x2.shape
    tm = COMB_TM
    per_seq = seq // tm
    pitch = slab + SUBLANES
    kern = functools.partial(_combine_kernel, tm=tm, slab=slab, pitch=pitch)
    grid_spec = pltpu.PrefetchScalarGridSpec(
        num_scalar_prefetch=1,
        grid=(t // tm,),
        in_specs=[
            pl.BlockSpec(memory_space=pl.ANY),
            pl.BlockSpec((tm, d), lambda i, sl: (i, 0)),
            pl.BlockSpec((tm, d), lambda i, sl: (i, 0)),
            pl.BlockSpec((1, 1, d), lambda i, sl: (i // per_seq, 0, 0)),
            pl.BlockSpec((tm, SUBLANES), lambda i, sl: (i, 0)),
        ],
        out_specs=pl.BlockSpec((tm, d), lambda i, sl: (i, 0)),
        scratch_shapes=[
            pltpu.VMEM((2 * TOP_K * tm * pitch, LANES), F32),
            pltpu.SemaphoreType.DMA((2,)),
        ],
    )
    return pl.pallas_call(
        kern,
        grid_spec=grid_spec,
        out_shape=jax.ShapeDtypeStruct((t, d), F32),
        compiler_params=_cparams(("arbitrary",)),
        name="gather_combine",
    )(slots_tiled, y_slots, shared, x2, gate, gate_w)


def _tile_slots(slot, tm):
    k, t = slot.shape
    return slot.reshape(k, t // tm, tm).transpose(1, 0, 2).reshape(-1)


def _moe_layer(x, g, sc, sh, gate, w_router, router_bias, w_gate, w_up, w_down,
               ws_gate_b, ws_up_b, ws_down_b, layer):
    b, s, d = x.shape
    t = b * s
    x2 = x.reshape(t, d)
    slab = d // LANES
    h, h_slabs, top_e, gates, rank, counts = _route(x2, g, sc, sh, w_router, router_bias, layer, s)

    blk = MOE_BLK
    n_blocks = -(-(t * TOP_K) // blk) + N_EXPERTS
    padded = (counts + blk - 1) // blk * blk
    pad_end = jnp.cumsum(padded).astype(I32)
    pad_start = pad_end - padded
    experts = jnp.arange(N_EXPERTS, dtype=I32)
    slot = jnp.sum(jnp.where(top_e[..., None] == experts, pad_start, 0), axis=-1) + rank
    block_first = jnp.arange(n_blocks, dtype=I32) * blk
    block_e = jnp.minimum(jnp.sum((pad_end[None, :] <= block_first[:, None]).astype(I32), axis=1), N_EXPERTS - 1)
    n_used = pad_end[-1:] // blk
    has_rows = padded > 0
    later = jnp.where(has_rows[None, :] & (experts[None, :] > experts[:, None]), experts[None, :], N_EXPERTS)
    following = jnp.min(later, axis=1)
    following = jnp.where(following < N_EXPERTS, following, -1)
    block_hot = block_e[:, None] == experts[None, :]
    next_expert = jnp.sum(jnp.where(block_hot, following[None, :], 0), axis=1).astype(I32)

    xs, shared = _dispatch(h_slabs, h, _tile_slots(slot, DISP_TM), pad_end, n_used, n_blocks, slab,
                           ws_gate_b, ws_up_b, ws_down_b, layer)
    y_slots = _experts(xs, w_gate, w_up, w_down, layer, block_e, next_expert, n_used, slab)
    gate_w = jnp.pad(gates.T, ((0, 0), (0, SUBLANES - TOP_K)))
    out = _combine(_tile_slots(slot, COMB_TM), y_slots, shared, x2, gate, gate_w, s, slab)
    return out.reshape(b, s, d)


def kernel(x, c, w_mod, b_mod, norm_mix, norm_ffn, w_in, w_out, q_norm, k_norm, lambda_q1, lambda_k1,
           lambda_q2, lambda_k2, subln, w_router, router_bias, w_gate, w_up, w_down, ws_gate, ws_up, ws_down):
    b, s, d = x.shape
    depth = w_mod.shape[0]
    assert d == SB_HEADS * HEAD_DIM == 2 * DA_HEADS * HEAD_DIM
    assert s % max(SB_TQ, SB_TK, DA_TQ, DA_TK, PROJ_TM, ROUTE_TM, DISP_TM) == 0
    assert SB_TQ % SB_TK == 0 and DA_TK % DA_TQ == 0
    q_scale = HEAD_DIM ** -0.5 * LOG2E

    mod = _modulation(c, w_mod, b_mod)
    ws_gate_b, ws_up_b, ws_down_b = ws_gate.astype(BF16), ws_up.astype(BF16), ws_down.astype(BF16)
    for i in range(depth):
        sh1, sc1, g1, sh2, sc2, g2 = [m.reshape(b, 1, d) for m in jnp.split(mod[i], 6, axis=-1)]
        w_in_b = w_in[i].astype(BF16)
        if i % 2 == 0:
            colscale = jnp.stack([jnp.full((1, d), q_scale, F32), jnp.ones((1, d), F32), jnp.ones((1, d), F32)])
            qk, v = _qkv_proj(x, norm_mix[i], sc1, sh1, w_in_b, colscale, qk_norm=False, v_width=HEAD_DIM)
            o = _sb_attention(qk, v, d)
        else:
            j = i // 2
            lambda_init = 0.8 - 0.6 * math.exp(-0.3 * i)
            reps = d // HEAD_DIM
            colscale = jnp.stack([
                jnp.tile(q_norm[j].astype(F32), reps)[None, :] * q_scale,
                jnp.tile(k_norm[j].astype(F32), reps)[None, :],
                jnp.ones((1, d), F32)])
            qk, v = _qkv_proj(x, norm_mix[i], sc1, sh1, w_in_b, colscale, qk_norm=True, v_width=2 * HEAD_DIM)
            lam_rows = jnp.stack([lambda_q1[j], lambda_k1[j], lambda_q2[j], lambda_k2[j]]).astype(F32)
            o = _da_attention(qk, v, lam_rows, subln[j].astype(F32), d, lambda_init)
        x = _out_proj(o, w_out[i].astype(BF16), x, g1)
        x = _moe_layer(x, norm_ffn[i], sc2, sh2, g2, w_router, router_bias, w_gate, w_up, w_down,
                       ws_gate_b, ws_up_b, ws_down_b, i)
    return x
```

```python
import functools
import math

import jax
import jax.numpy as jnp
from jax import lax
from jax.experimental import pallas as pl
from jax.experimental.pallas import tpu as pltpu

F32 = jnp.float32
BF16 = jnp.bfloat16
I32 = jnp.int32

LANES = 128
SUBLANES = 8
VMEM_LIMIT = 56 * 1024 * 1024

SB_HEADS = 16
DA_HEADS = 8
HEAD_DIM = 128
N_EXPERTS = 64
TOP_K = 6
N_GROUPS = 8
TOPK_GROUPS = 4
ROUTED_SCALE = 2.5
EPS = 1e-6
SUBLN_EPS = 1e-5
LOG2E = math.log2(math.e)

SB_TQ, SB_TK = 512, 256
DA_TQ, DA_TK = 1024, 1024
PROJ_TM = 512
ROUTE_TM = 512
MOE_BLK = 256
DISP_TM = 512
COMB_TM = 128


def _cparams(sem):
    return pltpu.CompilerParams(dimension_semantics=sem, vmem_limit_bytes=VMEM_LIMIT)


def _silu(x):
    return x * jax.nn.sigmoid(x)


def _dot(a, b):
    return jnp.dot(a, b, preferred_element_type=F32)


def _dot_nt(a, b):
    return lax.dot_general(a, b, (((1,), (1,)), ((), ())), preferred_element_type=F32)


def _store_slabs(ref, base, rows, vals):
    slab = vals.shape[1] // LANES
    for s in range(slab):
        ref[pl.ds(base + s, rows, stride=slab), :] = vals[:, s * LANES:(s + 1) * LANES]


def _load_slabs(ref, base, rows, slab, pitch=None):
    return [ref[pl.ds(base + s, rows, stride=pitch or slab), :] for s in range(slab)]


def _mod_kernel(c_ref, w_ref, b_ref, o_ref):
    cond = _silu(c_ref[...])
    o_ref[0] = _dot(cond.astype(BF16), w_ref[0].astype(BF16)) + b_ref[0]


def _modulation(c, w_mod, b_mod):
    depth, d, n = w_mod.shape
    b = c.shape[0]
    rows = -(-b // SUBLANES) * SUBLANES
    cp = jnp.pad(c, ((0, rows - b), (0, 0)))
    tn = 1024
    out = pl.pallas_call(
        _mod_kernel,
        grid=(depth, n // tn),
        in_specs=[
            pl.BlockSpec((rows, d), lambda l, j: (0, 0)),
            pl.BlockSpec((1, d, tn), lambda l, j: (l, 0, j)),
            pl.BlockSpec((1, 1, tn), lambda l, j: (l, 0, j)),
        ],
        out_specs=pl.BlockSpec((1, rows, tn), lambda l, j: (l, 0, j)),
        out_shape=jax.ShapeDtypeStruct((depth, rows, n), F32),
        compiler_params=_cparams(("arbitrary", "arbitrary")),
        name="adaln_mod",
    )(cp, w_mod, b_mod.reshape(depth, 1, n))
    return out[:, :b]


def _norm_mod(x, g, sc, sh):
    ms = jnp.mean(x * x, axis=-1, keepdims=True)
    return x * lax.rsqrt(ms + EPS) * g * (1.0 + sc) + sh


def _qkv_kernel(x_ref, g_ref, sc_ref, sh_ref, w_ref, cs_ref, qk_ref, v_ref, h_ref, *, qk_norm, v_width):
    j = pl.program_id(2)

    @pl.when(j == 0)
    def _():
        h_ref[...] = _norm_mod(x_ref[0], g_ref[...], sc_ref[0], sh_ref[0]).astype(BF16)

    acc = _dot(h_ref[...], w_ref[...])
    n_chunks = acc.shape[1] // HEAD_DIM

    @pl.when(j < 2)
    def _():
        for c in range(n_chunks):
            a = acc[:, c * HEAD_DIM:(c + 1) * HEAD_DIM]
            if qk_norm:
                a = a * lax.rsqrt(jnp.mean(a * a, axis=-1, keepdims=True) + EPS)
            a = a * cs_ref[0][:, c * HEAD_DIM:(c + 1) * HEAD_DIM]
            qk_ref[0, c] = a.astype(BF16)

    @pl.when(j == 2)
    def _():
        for c in range(acc.shape[1] // v_width):
            v_ref[0, c] = acc[:, c * v_width:(c + 1) * v_width].astype(BF16)


def _qkv_proj(x, g, sc, sh, w_bf16, colscale, *, qk_norm, v_width):
    b, s, d = x.shape
    tm = PROJ_TM
    n_ch = d // HEAD_DIM
    kern = functools.partial(_qkv_kernel, qk_norm=qk_norm, v_width=v_width)
    return pl.pallas_call(
        kern,
        grid=(b, s // tm, 3),
        in_specs=[
            pl.BlockSpec((1, tm, d), lambda bi, i, j: (bi, i, 0)),
            pl.BlockSpec((1, d), lambda bi, i, j: (0, 0)),
            pl.BlockSpec((1, 1, d), lambda bi, i, j: (bi, 0, 0)),
            pl.BlockSpec((1, 1, d), lambda bi, i, j: (bi, 0, 0)),
            pl.BlockSpec((d, d), lambda bi, i, j: (0, j)),
            pl.BlockSpec((1, 1, d), lambda bi, i, j: (j, 0, 0)),
        ],
        out_specs=[
            pl.BlockSpec((1, n_ch, tm, HEAD_DIM), lambda bi, i, j: (bi, jnp.minimum(j, 1), i, 0)),
            pl.BlockSpec((1, d // v_width, tm, v_width), lambda bi, i, j: (bi, 0, i, 0)),
        ],
        out_shape=[
            jax.ShapeDtypeStruct((b, 2 * n_ch, s, HEAD_DIM), BF16),
            jax.ShapeDtypeStruct((b, d // v_width, s, v_width), BF16),
        ],
        scratch_shapes=[pltpu.VMEM((tm, d), BF16)],
        compiler_params=_cparams(("arbitrary", "arbitrary", "arbitrary")),
        name="norm_qkv_proj",
    )(x, g.reshape(1, d), sc, sh, w_bf16, colscale)


def _sb_kernel(q_ref, k_ref, v_ref, u_ref, o_ref, *, tq, tk):
    i = pl.program_id(2)
    q = q_ref[0, 0]
    per = tq // tk
    rows = i * tq + lax.broadcasted_iota(I32, (tq, 1), 0)
    u = u_ref[...]


    def weights(c, r0, run, masked):
        start = pl.multiple_of(c * tk, tk)
        z = _dot_nt(q[r0:], k_ref[0, 0, pl.ds(start, tk), :])
        softplus = jnp.maximum(z, 0.0) + jnp.log(1.0 + jnp.exp2(-jnp.abs(z))) * LOG2E
        if masked:
            earlier = (start + lax.broadcasted_iota(I32, (1, tk), 1)) < rows[r0:]
            softplus = jnp.where(earlier, softplus, 0.0)
        between = _dot(softplus.astype(BF16), u) + run[r0:]
        a = jnp.exp2((z - softplus) + between)
        if masked:
            a = jnp.where(earlier, a, 0.0)
        dec = jnp.sum(softplus, axis=-1, keepdims=True)
        run = jnp.concatenate([run[:r0], run[r0:] - dec], axis=0) if r0 else run - dec
        return a.astype(BF16), run

    def values(c, r0, a, acc):
        pv = _dot(a, v_ref[0, 0, pl.ds(pl.multiple_of(c * tk, tk), tk), :])
        return jnp.concatenate([acc[:r0], acc[r0:] + pv], axis=0) if r0 else acc + pv

    def group(chunks, run, acc, masked):
        ws = []
        for c, r0 in chunks:
            a, run = weights(c, r0, run, masked)
            ws.append(a)
        for (c, r0), a in zip(chunks, ws):
            acc = values(c, r0, a, acc)
        return run, acc

    run, acc = group([(per * i + m, m * tk) for m in reversed(range(per))],
                     jnp.zeros((tq, 1), F32), jnp.zeros((tq, HEAD_DIM), F32), True)
    def sweep(top, n, carry):
        return group([(top - r, 0) for r in range(n)], carry[0], carry[1], False)

    odd = i % 2
    top = per * i - 1
    carry = lax.fori_loop(0, odd, lambda _, cr: sweep(top, per, cr), (run, acc))
    top = top - per * odd
    _, acc = lax.fori_loop(0, i // 2, lambda b, cr: sweep(top - b * (2 * per), 2 * per, cr), carry)
    o_ref[0] = acc.astype(BF16)


def _sb_attention(qk, v, d):
    b, _, s, _ = qk.shape
    tq, tk = SB_TQ, SB_TK
    heads = SB_HEADS
    jj = lax.broadcasted_iota(I32, (tk, tk), 0)
    ss = lax.broadcasted_iota(I32, (tk, tk), 1)
    u = jnp.where(jj > ss, -1.0, 0.0).astype(BF16)
    kern = functools.partial(_sb_kernel, tq=tq, tk=tk)
    return pl.pallas_call(
        kern,
        grid=(b, heads, s // tq),
        in_specs=[
            pl.BlockSpec((1, 1, tq, HEAD_DIM), lambda bi, h, i: (bi, h, i, 0)),
            pl.BlockSpec((1, 1, s, HEAD_DIM), lambda bi, h, i: (bi, heads + h, 0, 0)),
            pl.BlockSpec((1, 1, s, HEAD_DIM), lambda bi, h, i: (bi, h, 0, 0)),
            pl.BlockSpec((tk, tk), lambda bi, h, i: (0, 0)),
        ],
        out_specs=pl.BlockSpec((1, tq, HEAD_DIM), lambda bi, h, i: (bi, i, h)),
        out_shape=jax.ShapeDtypeStruct((b, s, d), BF16),
        compiler_params=_cparams(("arbitrary", "arbitrary", "arbitrary")),
        name="stickbreak_attn",
    )(qk, qk, v, u)


def _da_kernel(q_ref, k_ref, v_ref, slope_ref, lam_ref, subln_ref, o_ref, *, tq, tk, lambda_init):
    i = pl.program_id(2)
    row0 = i * tq
    rows = row0 + lax.broadcasted_iota(I32, (tq, 1), 0)
    slope = slope_ref[0]
    qs = (q_ref[0, 0], q_ref[0, 1])
    dv = v_ref.shape[-1]

    def chunk(start, width, carry, masked):
        start = pl.multiple_of(start, tq)
        cols = start + lax.broadcasted_iota(I32, (1, width), 1)
        key_bias = slope * (cols - row0).astype(F32)
        v = v_ref[0, 0, pl.ds(start, width), :]
        new = []
        for m in range(2):
            mx, den, acc = carry[3 * m:3 * m + 3]
            k = k_ref[0, m, pl.ds(start, width), :]
            z = _dot_nt(qs[m], k) + key_bias
            if masked:
                z = jnp.where(cols <= rows, z, -jnp.inf)
            mx_new = jnp.maximum(mx, jnp.max(z, axis=-1, keepdims=True))
            alpha = jnp.exp2(mx - mx_new)
            p = jnp.exp2(z - mx_new)
            den = alpha * den + jnp.sum(p, axis=-1, keepdims=True)
            acc = alpha * acc + _dot(p.astype(BF16), v)
            new += [mx_new, den, acc]
        return tuple(new)

    init = []
    for _ in range(2):
        init += [jnp.full((tq, 1), -jnp.inf, F32), jnp.zeros((tq, 1), F32), jnp.zeros((tq, dv), F32)]
    n_below = row0 // tk
    carry = lax.fori_loop(0, n_below, lambda c, cr: chunk(c * tk, tk, cr, False), tuple(init))
    carry = chunk(n_below * tk, tk, carry, True)
    _, l1, a1, _, l2, a2 = carry

    lv = lam_ref[...]
    s1 = jnp.sum(lv[0:1] * lv[1:2], axis=-1, keepdims=True)
    s2 = jnp.sum(lv[2:3] * lv[3:4], axis=-1, keepdims=True)
    lam = jnp.exp(s1) - jnp.exp(s2) + lambda_init
    o = a1 / l1 - lam * (a2 / l2)
    o = o * lax.rsqrt(jnp.mean(o * o, axis=-1, keepdims=True) + SUBLN_EPS) * subln_ref[...]
    o_ref[0] = (o * (1.0 - lambda_init)).astype(BF16)


def _da_attention(qk, v, lam_rows, subln, d, lambda_init):
    b, _, s, _ = qk.shape
    tq, tk = DA_TQ, DA_TK
    heads = DA_HEADS
    dv = v.shape[-1]
    slopes = jnp.asarray(
        [2.0 ** (-8.0 * (h + 1) / heads) * LOG2E for h in range(heads)], F32).reshape(heads, 1, 1)
    kern = functools.partial(_da_kernel, tq=tq, tk=tk, lambda_init=lambda_init)
    return pl.pallas_call(
        kern,
        grid=(b, heads, s // tq),
        in_specs=[
            pl.BlockSpec((1, 2, tq, HEAD_DIM), lambda bi, h, i: (bi, h, i, 0)),
            pl.BlockSpec((1, 2, s, HEAD_DIM), lambda bi, h, i: (bi, heads + h, 0, 0)),
            pl.BlockSpec((1, 1, s, dv), lambda bi, h, i: (bi, h, 0, 0)),
            pl.BlockSpec((1, 1, 1), lambda bi, h, i: (h, 0, 0)),
            pl.BlockSpec((4, HEAD_DIM), lambda bi, h, i: (0, 0)),
            pl.BlockSpec((1, dv), lambda bi, h, i: (0, 0)),
        ],
        out_specs=pl.BlockSpec((1, tq, dv), lambda bi, h, i: (bi, i, h)),
        out_shape=jax.ShapeDtypeStruct((b, s, d), BF16),
        compiler_params=_cparams(("arbitrary", "arbitrary", "arbitrary")),
        name="diff_attn",
    )(qk, qk, v, slopes, lam_rows, subln.reshape(1, dv))


def _outproj_kernel(o_ref, w_ref, x_ref, g_ref, y_ref):
    y_ref[0] = x_ref[0] + g_ref[0] * _dot(o_ref[0], w_ref[...])


def _out_proj(o, w_bf16, x, gate):
    b, s, d = x.shape
    tm = PROJ_TM
    return pl.pallas_call(
        _outproj_kernel,
        grid=(b, s // tm),
        in_specs=[
            pl.BlockSpec((1, tm, d), lambda bi, i: (bi, i, 0)),
            pl.BlockSpec((d, d), lambda bi, i: (0, 0)),
            pl.BlockSpec((1, tm, d), lambda bi, i: (bi, i, 0)),
            pl.BlockSpec((1, 1, d), lambda bi, i: (bi, 0, 0)),
        ],
        out_specs=pl.BlockSpec((1, tm, d), lambda bi, i: (bi, i, 0)),
        out_shape=jax.ShapeDtypeStruct((b, s, d), F32),
        compiler_params=_cparams(("arbitrary", "arbitrary")),
        name="out_proj_residual",
    )(o, w_bf16, x, gate)


def _router_kernel(x_ref, g_ref, sc_ref, sh_ref, wr_ref, bias_ref, u_ref,
                   h_ref, slab_ref, te_ref, gate_ref, rank_ref, cnt_ref, run_ref, *, tm):
    i = pl.program_id(0)
    per_group = N_EXPERTS // N_GROUPS

    @pl.when(i == 0)
    def _():
        run_ref[...] = jnp.zeros_like(run_ref)

    h = _norm_mod(x_ref[...], g_ref[...], sc_ref[0], sh_ref[0])
    h_ref[...] = h.astype(BF16)
    _store_slabs(slab_ref, 0, tm, h)
    logits = lax.dot_general(wr_ref[0], h, (((1,), (1,)), ((), ())),
                             precision=lax.Precision.HIGHEST, preferred_element_type=F32)
    scores = jax.nn.sigmoid(logits)
    biased = scores + bias_ref[0]
    neg = -jnp.inf
    sub = lax.broadcasted_iota(I32, (per_group, tm), 0)

    def first_max(vals, idx, sentinel):
        m = jnp.max(vals, axis=0, keepdims=True)
        return m, jnp.min(jnp.where(vals == m, idx, sentinel), axis=0, keepdims=True)

    sc_g = [scores[g * per_group:(g + 1) * per_group] for g in range(N_GROUPS)]
    bi_g = [biased[g * per_group:(g + 1) * per_group] for g in range(N_GROUPS)]

    gscore = jnp.zeros((N_GROUPS, tm), F32)
    for g in range(N_GROUPS):
        m1, i1 = first_max(bi_g[g], sub, per_group)
        m2 = jnp.max(jnp.where(sub == i1, neg, bi_g[g]), axis=0, keepdims=True)
        gscore = jnp.where(sub == g, m1 + m2, gscore)

    keep = jnp.zeros((N_GROUPS, tm), F32)
    cur = gscore
    for _ in range(TOPK_GROUPS):
        _, gi = first_max(cur, sub, N_GROUPS)
        hit = sub == gi
        keep = jnp.where(hit, 1.0, keep)
        cur = jnp.where(hit, neg, cur)

    masked = []
    for g in range(N_GROUPS):
        kg = jnp.max(jnp.where(sub == g, keep, 0.0), axis=0, keepdims=True)
        masked.append(jnp.where(kg > 0.0, bi_g[g], neg))
    eidx = [sub + g * per_group for g in range(N_GROUPS)]
    sel = [jnp.zeros((per_group, tm), F32) for _ in range(N_GROUPS)]

    top_e, gates = [], []
    for _ in range(TOP_K):
        m = masked[0].max(axis=0, keepdims=True)
        for g in range(1, N_GROUPS):
            m = jnp.maximum(m, jnp.max(masked[g], axis=0, keepdims=True))
        ei = jnp.full((1, tm), N_EXPERTS, I32)
        for g in range(N_GROUPS):
            ei = jnp.minimum(ei, jnp.min(jnp.where(masked[g] == m, eidx[g], N_EXPERTS), axis=0, keepdims=True))
        gt = jnp.zeros((1, tm), F32)
        for g in range(N_GROUPS):
            hit = eidx[g] == ei
            gt = gt + jnp.sum(jnp.where(hit, sc_g[g], 0.0), axis=0, keepdims=True)
            masked[g] = jnp.where(hit, neg, masked[g])
            sel[g] = jnp.where(hit, 1.0, sel[g])
        top_e.append(ei)
        gates.append(gt)

    gsum = gates[0]
    for gt in gates[1:]:
        gsum = gsum + gt

    sel_all = jnp.concatenate(sel, axis=0)
    before = _dot(sel_all.astype(BF16), u_ref[...]) + run_ref[...]
    run_new = run_ref[...] + jnp.sum(sel_all, axis=1, keepdims=True)
    run_ref[...] = run_new
    cnt_ref[...] = jnp.broadcast_to(run_new, cnt_ref.shape).astype(I32)

    te_ref[...] = jnp.zeros_like(te_ref)
    gate_ref[...] = jnp.zeros_like(gate_ref)
    rank_ref[...] = jnp.zeros_like(rank_ref)
    for j in range(TOP_K):
        rk = jnp.zeros((1, tm), F32)
        for g in range(N_GROUPS):
            rk = rk + jnp.sum(jnp.where(eidx[g] == top_e[j], before[g * per_group:(g + 1) * per_group], 0.0),
                              axis=0, keepdims=True)
        te_ref[j:j + 1, :] = top_e[j]
        gate_ref[j:j + 1, :] = gates[j] / gsum * ROUTED_SCALE
        rank_ref[j:j + 1, :] = rk.astype(I32)


def _route(x2, g, sc, sh, w_router, bias, layer, seq):
    t, d = x2.shape
    tm = ROUTE_TM
    e = N_EXPERTS
    slab = d // LANES
    jj = lax.broadcasted_iota(I32, (tm, tm), 0)
    ss = lax.broadcasted_iota(I32, (tm, tm), 1)
    u = (jj < ss).astype(BF16)
    per_seq = seq // tm
    kern = functools.partial(_router_kernel, tm=tm)
    rows = SUBLANES
    depth = w_router.shape[0]
    h, h_slabs, te, gate, rank, cnt = pl.pallas_call(
        kern,
        grid=(t // tm,),
        in_specs=[
            pl.BlockSpec((tm, d), lambda i: (i, 0)),
            pl.BlockSpec((1, d), lambda i: (0, 0)),
            pl.BlockSpec((1, 1, d), lambda i: (i // per_seq, 0, 0)),
            pl.BlockSpec((1, 1, d), lambda i: (i // per_seq, 0, 0)),
            pl.BlockSpec((1, e, d), lambda i: (layer, 0, 0)),
            pl.BlockSpec((1, e, 1), lambda i: (layer, 0, 0)),
            pl.BlockSpec((tm, tm), lambda i: (0, 0)),
        ],
        out_specs=[
            pl.BlockSpec((tm, d), lambda i: (i, 0)),
            pl.BlockSpec((tm * slab, LANES), lambda i: (i, 0)),
            pl.BlockSpec((rows, tm), lambda i: (0, i)),
            pl.BlockSpec((rows, tm), lambda i: (0, i)),
            pl.BlockSpec((rows, tm), lambda i: (0, i)),
            pl.BlockSpec((e, LANES), lambda i: (0, 0)),
        ],
        out_shape=[
            jax.ShapeDtypeStruct((t, d), BF16),
            jax.ShapeDtypeStruct((t * slab, LANES), F32),
            jax.ShapeDtypeStruct((rows, t), I32),
            jax.ShapeDtypeStruct((rows, t), F32),
            jax.ShapeDtypeStruct((rows, t), I32),
            jax.ShapeDtypeStruct((e, LANES), I32),
        ],
        scratch_shapes=[pltpu.VMEM((e, 1), F32)],
        compiler_params=_cparams(("arbitrary",)),
        name="norm_router_topk",
    )(x2, g.reshape(1, d), sc, sh, w_router, bias.reshape(depth, e, 1), u)
    return h, h_slabs, te[:TOP_K], gate[:TOP_K], rank[:TOP_K], cnt[:, 0]


def _dispatch_kernel(slot_ref, pe_ref, nu_ref, h_ref, hb_ref, wg_ref, wu_ref, wd_ref, xs_hbm, sh_ref,
                     zbuf, sem, zsem, *, tm, blk, n_blocks, slab):
    i = pl.program_id(0)
    n_used = nu_ref[0]
    group = 8

    def zero_copy(block):
        start = pl.multiple_of(block * (blk * slab), blk * slab)
        return pltpu.make_async_copy(zbuf, xs_hbm.at[pl.ds(start, blk * slab)], zsem)

    def expert_has_rows(e):
        return pe_ref[e] > (pe_ref[e - 1] if e else 0)

    @pl.when(i == 0)
    def _():
        zbuf[...] = jnp.zeros_like(zbuf)
        for phase in ("start", "wait"):
            for e in range(N_EXPERTS):
                @pl.when(expert_has_rows(e))
                def _():
                    cp = zero_copy(pe_ref[e] // blk - 1)
                    cp.start() if phase == "start" else cp.wait()

            def tail(bk, carry):
                cp = zero_copy(bk)
                cp.start() if phase == "start" else cp.wait()
                return carry

            lax.fori_loop(n_used, n_blocks, tail, 0)

    base = i * (TOP_K * tm)
    for j in range(TOP_K):
        def body(g, carry):
            for r in range(group):
                dst = pl.multiple_of(slot_ref[base + j * tm + g * group + r] * slab, slab)
                src = pl.multiple_of(g * (group * slab), group * slab) + r * slab
                pltpu.make_async_copy(h_ref.at[pl.ds(src, slab)], xs_hbm.at[pl.ds(dst, slab)], sem).start(
                    priority=r % 2)
            return carry

        lax.fori_loop(0, tm // group, body, 0)

    hb = hb_ref[...]
    mid = _silu(_dot(hb, wg_ref[0])) * _dot(hb, wu_ref[0])
    sh_ref[...] = _dot(mid.astype(BF16), wd_ref[0]).astype(BF16)

    for j in range(TOP_K):
        pltpu.make_async_copy(h_ref, xs_hbm.at[pl.ds(0, tm * slab)], sem).wait()


def _dispatch(h_slabs, h, slots_tiled, pad_end, n_used, n_blocks, slab, ws_gate, ws_up, ws_down, layer):
    t, d = h.shape
    f = ws_gate.shape[-1]
    tm = DISP_TM
    blk = MOE_BLK
    kern = functools.partial(_dispatch_kernel, tm=tm, blk=blk, n_blocks=n_blocks, slab=slab)
    grid_spec = pltpu.PrefetchScalarGridSpec(
        num_scalar_prefetch=3,
        grid=(t // tm,),
        in_specs=[
            pl.BlockSpec((tm * slab, LANES), lambda i, sl, pe, nu: (i, 0)),
            pl.BlockSpec((tm, d), lambda i, sl, pe, nu: (i, 0)),
            pl.BlockSpec((1, d, f), lambda i, sl, pe, nu: (layer, 0, 0)),
            pl.BlockSpec((1, d, f), lambda i, sl, pe, nu: (layer, 0, 0)),
            pl.BlockSpec((1, f, d), lambda i, sl, pe, nu: (layer, 0, 0)),
        ],
        out_specs=[
            pl.BlockSpec(memory_space=pl.ANY),
            pl.BlockSpec((tm, d), lambda i, sl, pe, nu: (i, 0)),
        ],
        scratch_shapes=[
            pltpu.VMEM((blk * slab, LANES), F32),
            pltpu.SemaphoreType.DMA,
            pltpu.SemaphoreType.DMA,
        ],
    )
    return pl.pallas_call(
        kern,
        grid_spec=grid_spec,
        out_shape=[
            jax.ShapeDtypeStruct((n_blocks * blk * slab, LANES), F32),
            jax.ShapeDtypeStruct((t, d), BF16),
        ],
        compiler_params=_cparams(("arbitrary",)),
        name="dispatch_rows",
    )(slots_tiled, pad_end, n_used, h_slabs, h, ws_gate, ws_up, ws_down)


def _expert_kernel(be_ref, ne_ref, nu_ref, x_ref, wg_hbm, wu_hbm, wd_hbm, y_ref,
                   wg32, wu32, wd32, wgb, wub, wdb, sem, *, blk, slab, layer):
    i = pl.program_id(0)
    n_used = nu_ref[0]

    def weight_copies(e):
        return [pltpu.make_async_copy(src.at[layer, e], dst, sem)
                for src, dst in ((wg_hbm, wg32), (wu_hbm, wu32), (wd_hbm, wd32))]

    @pl.when(i == 0)
    def _():
        for cp in weight_copies(be_ref[0]):
            cp.start()

    @pl.when(i < n_used)
    def _():
        prev = be_ref[jnp.maximum(i - 1, 0)]

        @pl.when((i == 0) | (be_ref[i] != prev))
        def _():
            for cp in weight_copies(be_ref[i]):
                cp.wait()
            wgb[...] = wg32[...].astype(BF16)
            wub[...] = wu32[...].astype(BF16)
            wdb[...] = wd32[...].astype(BF16)
            nxt = ne_ref[i]

            @pl.when(nxt >= 0)
            def _():
                for cp in weight_copies(nxt):
                    cp.start()

        x = jnp.concatenate([c.astype(BF16) for c in _load_slabs(x_ref, 0, blk, slab)], axis=1)
        mid = _silu(_dot(x, wgb[...])) * _dot(x, wub[...])
        _store_slabs(y_ref, 0, blk, _dot(mid.astype(BF16), wdb[...]))

    @pl.when(i >= n_used)
    def _():
        y_ref[...] = jnp.zeros_like(y_ref)


def _experts(xs, w_gate, w_up, w_down, layer, block_e, next_expert, n_used, slab):
    d, f = w_gate.shape[-2:]
    blk = MOE_BLK
    n_blocks = xs.shape[0] // (blk * slab)
    kern = functools.partial(_expert_kernel, blk=blk, slab=slab, layer=layer)
    grid_spec = pltpu.PrefetchScalarGridSpec(
        num_scalar_prefetch=3,
        grid=(n_blocks,),
        in_specs=[
            pl.BlockSpec((blk * slab, LANES), lambda i, be, ne, nu: (jnp.minimum(i, nu[0] - 1), 0)),
            pl.BlockSpec(memory_space=pl.ANY),
            pl.BlockSpec(memory_space=pl.ANY),
            pl.BlockSpec(memory_space=pl.ANY),
        ],
        out_specs=pl.BlockSpec((blk * slab, LANES), lambda i, be, ne, nu: (i, 0)),
        scratch_shapes=[
            pltpu.VMEM((d, f), F32),
            pltpu.VMEM((d, f), F32),
            pltpu.VMEM((f, d), F32),
            pltpu.VMEM((d, f), BF16),
            pltpu.VMEM((d, f), BF16),
            pltpu.VMEM((f, d), BF16),
            pltpu.SemaphoreType.DMA,
        ],
    )
    return pl.pallas_call(
        kern,
        grid_spec=grid_spec,
        out_shape=jax.ShapeDtypeStruct(xs.shape, F32),
        compiler_params=_cparams(("arbitrary",)),
        name="routed_experts",
    )(block_e, next_expert, n_used, xs, w_gate, w_up, w_down)


def _combine_kernel(slot_ref, y_hbm, sh_ref, x_ref, g_ref, gw_ref, o_ref, ybuf, sem, *, tm, slab, pitch):
    i = pl.program_id(0)
    n = pl.num_programs(0)
    rows = TOP_K * tm
    group = 8

    def issue(tile, slot):
        base = tile * rows

        def body(g, carry):
            for r in range(group):
                src = pl.multiple_of(slot_ref[base + g * group + r] * slab, slab)
                dst = pl.multiple_of((slot * rows + g * group) * pitch, group * pitch) + r * pitch
                pltpu.make_async_copy(y_hbm.at[pl.ds(src, slab)], ybuf.at[pl.ds(dst, slab)], sem.at[slot]).start(
                    priority=r % 2)
            return carry

        lax.fori_loop(0, rows // group, body, 0)

    @pl.when(i == 0)
    def _():
        issue(0, 0)

    @pl.when(i + 1 < n)
    def _():
        issue(i + 1, (i + 1) % 2)

    slot = i % 2
    buf0 = pl.multiple_of(slot * (rows * pitch), rows * pitch)
    pltpu.make_async_copy(y_hbm.at[pl.ds(0, rows * slab)], ybuf.at[pl.ds(0, rows * slab)], sem.at[slot]).wait()
    gw = gw_ref[...]
    gwb = [jnp.broadcast_to(gw[:, j:j + 1], (tm, LANES)) for j in range(TOP_K)]
    chunks = [jnp.zeros((tm, LANES), F32) for _ in range(slab)]
    for j in range(TOP_K):
        for s, y in enumerate(_load_slabs(ybuf, buf0 + j * (tm * pitch), tm, slab, pitch)):
            chunks[s] = chunks[s] + gwb[j] * y
    routed = jnp.concatenate(chunks, axis=1)
    o_ref[...] = x_ref[...] + g_ref[0] * (sh_ref[...].astype(F32) + routed)


def _combine(slots_tiled, y_slots, shared, x2, gate, gate_w, seq, slab):
    t, d = x2.shape
    tm = COMB_TM
    per_seq = seq // tm
    pitch = slab + SUBLANES
    kern = functools.partial(_combine_kernel, tm=tm, slab=slab, pitch=pitch)
    grid_spec = pltpu.PrefetchScalarGridSpec(
        num_scalar_prefetch=1,
        grid=(t // tm,),
        in_specs=[
            pl.BlockSpec(memory_space=pl.ANY),
            pl.BlockSpec((tm, d), lambda i, sl: (i, 0)),
            pl.BlockSpec((tm, d), lambda i, sl: (i, 0)),
            pl.BlockSpec((1, 1, d), lambda i, sl: (i // per_seq, 0, 0)),
            pl.BlockSpec((tm, SUBLANES), lambda i, sl: (i, 0)),
        ],
        out_specs=pl.BlockSpec((tm, d), lambda i, sl: (i, 0)),
        scratch_shapes=[
            pltpu.VMEM((2 * TOP_K * tm * pitch, LANES), F32),
            pltpu.SemaphoreType.DMA((2,)),
        ],
    )
    return pl.pallas_call(
        kern,
        grid_spec=grid_spec,
        out_shape=jax.ShapeDtypeStruct((t, d), F32),
        compiler_params=_cparams(("arbitrary",)),
        name="gather_combine",
    )(slots_tiled, y_slots, shared, x2, gate, gate_w)


def _tile_slots(slot, tm):
    k, t = slot.shape
    return slot.reshape(k, t // tm, tm).transpose(1, 0, 2).reshape(-1)


def _moe_layer(x, g, sc, sh, gate, w_router, router_bias, w_gate, w_up, w_down,
               ws_gate_b, ws_up_b, ws_down_b, layer):
    b, s, d = x.shape
    t = b * s
    x2 = x.reshape(t, d)
    slab = d // LANES
    h, h_slabs, top_e, gates, rank, counts = _route(x2, g, sc, sh, w_router, router_bias, layer, s)

    blk = MOE_BLK
    n_blocks = -(-(t * TOP_K) // blk) + N_EXPERTS
    padded = (counts + blk - 1) // blk * blk
    pad_end = jnp.cumsum(padded).astype(I32)
    pad_start = pad_end - padded
    experts = jnp.arange(N_EXPERTS, dtype=I32)
    slot = jnp.sum(jnp.where(top_e[..., None] == experts, pad_start, 0), axis=-1) + rank
    block_first = jnp.arange(n_blocks, dtype=I32) * blk
    block_e = jnp.minimum(jnp.sum((pad_end[None, :] <= block_first[:, None]).astype(I32), axis=1), N_EXPERTS - 1)
    n_used = pad_end[-1:] // blk
    has_rows = padded > 0
    later = jnp.where(has_rows[None, :] & (experts[None, :] > experts[:, None]), experts[None, :], N_EXPERTS)
    following = jnp.min(later, axis=1)
    following = jnp.where(following < N_EXPERTS, following, -1)
    block_hot = block_e[:, None] == experts[None, :]
    next_expert = jnp.sum(jnp.where(block_hot, following[None, :], 0), axis=1).astype(I32)

    xs, shared = _dispatch(h_slabs, h, _tile_slots(slot, DISP_TM), pad_end, n_used, n_blocks, slab,
                           ws_gate_b, ws_up_b, ws_down_b, layer)
    y_slots = _experts(xs, w_gate, w_up, w_down, layer, block_e, next_expert, n_used, slab)
    gate_w = jnp.pad(gates.T, ((0, 0), (0, SUBLANES - TOP_K)))
    out = _combine(_tile_slots(slot, COMB_TM), y_slots, shared, x2, gate, gate_w, s, slab)
    return out.reshape(b, s, d)


def kernel(x, c, w_mod, b_mod, norm_mix, norm_ffn, w_in, w_out, q_norm, k_norm, lambda_q1, lambda_k1,
           lambda_q2, lambda_k2, subln, w_router, router_bias, w_gate, w_up, w_down, ws_gate, ws_up, ws_down):
    b, s, d = x.shape
    depth = w_mod.shape[0]
    assert d == SB_HEADS * HEAD_DIM == 2 * DA_HEADS * HEAD_DIM
    assert s % max(SB_TQ, SB_TK, DA_TQ, DA_TK, PROJ_TM, ROUTE_TM, DISP_TM) == 0
    assert SB_TQ % SB_TK == 0 and DA_TK % DA_TQ == 0
    q_scale = HEAD_DIM ** -0.5 * LOG2E

    mod = _modulation(c, w_mod, b_mod)
    ws_gate_b, ws_up_b, ws_down_b = ws_gate.astype(BF16), ws_up.astype(BF16), ws_down.astype(BF16)
    for i in range(depth):
        sh1, sc1, g1, sh2, sc2, g2 = [m.reshape(b, 1, d) for m in jnp.split(mod[i], 6, axis=-1)]
        w_in_b = w_in[i].astype(BF16)
        if i % 2 == 0:
            colscale = jnp.stack([jnp.full((1, d), q_scale, F32), jnp.ones((1, d), F32), jnp.ones((1, d), F32)])
            qk, v = _qkv_proj(x, norm_mix[i], sc1, sh1, w_in_b, colscale, qk_norm=False, v_width=HEAD_DIM)
            o = _sb_attention(qk, v, d)
        else:
            j = i // 2
            lambda_init = 0.8 - 0.6 * math.exp(-0.3 * i)
            reps = d // HEAD_DIM
            colscale = jnp.stack([
                jnp.tile(q_norm[j].astype(F32), reps)[None, :] * q_scale,
                jnp.tile(k_norm[j].astype(F32), reps)[None, :],
                jnp.ones((1, d), F32)])
            qk, v = _qkv_proj(x, norm_mix[i], sc1, sh1, w_in_b, colscale, qk_norm=True, v_width=2 * HEAD_DIM)
            lam_rows = jnp.stack([lambda_q1[j], lambda_k1[j], lambda_q2[j], lambda_k2[j]]).astype(F32)
            o = _da_attention(qk, v, lam_rows, subln[j].astype(F32), d, lambda_init)
        x = _out_proj(o, w_out[i].astype(BF16), x, g1)
        x = _moe_layer(x, norm_ffn[i], sc2, sh2, g2, w_router, router_bias, w_gate, w_up, w_down,
                       ws_gate_b, ws_up_b, ws_down_b, i)
    return x
```

```python
import functools
import math

import jax
import jax.numpy as jnp
from jax import lax
from jax.experimental import pallas as pl
from jax.experimental.pallas import tpu as pltpu

F32 = jnp.float32
BF16 = jnp.bfloat16
I32 = jnp.int32

LANES = 128
SUBLANES = 8
VMEM_LIMIT = 56 * 1024 * 1024

SB_HEADS = 16
DA_HEADS = 8
HEAD_DIM = 128
N_EXPERTS = 64
TOP_K = 6
N_GROUPS = 8
TOPK_GROUPS = 4
ROUTED_SCALE = 2.5
EPS = 1e-6
SUBLN_EPS = 1e-5
LOG2E = math.log2(math.e)

SB_TQ, SB_TK = 512, 256
DA_TQ, DA_TK = 1024, 1024
PROJ_TM = 512
ROUTE_TM = 512
MOE_BLK = 256
DISP_TM = 512
COMB_TM = 128


def _cparams(sem):
    return pltpu.CompilerParams(dimension_semantics=sem, vmem_limit_bytes=VMEM_LIMIT)


def _silu(x):
    return x * jax.nn.sigmoid(x)


def _dot(a, b):
    return jnp.dot(a, b, preferred_element_type=F32)


def _dot_nt(a, b):
    return lax.dot_general(a, b, (((1,), (1,)), ((), ())), preferred_element_type=F32)


def _store_slabs(ref, base, rows, vals):
    slab = vals.shape[1] // LANES
    for s in range(slab):
        ref[pl.ds(base + s, rows, stride=slab), :] = vals[:, s * LANES:(s + 1) * LANES]


def _load_slabs(ref, base, rows, slab, pitch=None):
    return [ref[pl.ds(base + s, rows, stride=pitch or slab), :] for s in range(slab)]


def _mod_kernel(c_ref, w_ref, b_ref, o_ref):
    cond = _silu(c_ref[...])
    o_ref[0] = _dot(cond.astype(BF16), w_ref[0].astype(BF16)) + b_ref[0]


def _modulation(c, w_mod, b_mod):
    depth, d, n = w_mod.shape
    b = c.shape[0]
    rows = -(-b // SUBLANES) * SUBLANES
    cp = jnp.pad(c, ((0, rows - b), (0, 0)))
    tn = 1024
    out = pl.pallas_call(
        _mod_kernel,
        grid=(depth, n // tn),
        in_specs=[
            pl.BlockSpec((rows, d), lambda l, j: (0, 0)),
            pl.BlockSpec((1, d, tn), lambda l, j: (l, 0, j)),
            pl.BlockSpec((1, 1, tn), lambda l, j: (l, 0, j)),
        ],
        out_specs=pl.BlockSpec((1, rows, tn), lambda l, j: (l, 0, j)),
        out_shape=jax.ShapeDtypeStruct((depth, rows, n), F32),
        compiler_params=_cparams(("arbitrary", "arbitrary")),
        name="adaln_mod",
    )(cp, w_mod, b_mod.reshape(depth, 1, n))
    return out[:, :b]


def _norm_mod(x, g, sc, sh):
    ms = jnp.mean(x * x, axis=-1, keepdims=True)
    return x * lax.rsqrt(ms + EPS) * g * (1.0 + sc) + sh


def _qkv_kernel(x_ref, g_ref, sc_ref, sh_ref, w_ref, cs_ref, qk_ref, v_ref, h_ref, *, qk_norm, v_width):
    j = pl.program_id(2)

    @pl.when(j == 0)
    def _():
        h_ref[...] = _norm_mod(x_ref[0], g_ref[...], sc_ref[0], sh_ref[0]).astype(BF16)

    acc = _dot(h_ref[...], w_ref[...])
    n_chunks = acc.shape[1] // HEAD_DIM

    @pl.when(j < 2)
    def _():
        for c in range(n_chunks):
            a = acc[:, c * HEAD_DIM:(c + 1) * HEAD_DIM]
            if qk_norm:
                a = a * lax.rsqrt(jnp.mean(a * a, axis=-1, keepdims=True) + EPS)
            a = a * cs_ref[0][:, c * HEAD_DIM:(c + 1) * HEAD_DIM]
            qk_ref[0, c] = a.astype(BF16)

    @pl.when(j == 2)
    def _():
        for c in range(acc.shape[1] // v_width):
            v_ref[0, c] = acc[:, c * v_width:(c + 1) * v_width].astype(BF16)


def _qkv_proj(x, g, sc, sh, w_bf16, colscale, *, qk_norm, v_width):
    b, s, d = x.shape
    tm = PROJ_TM
    n_ch = d // HEAD_DIM
    kern = functools.partial(_qkv_kernel, qk_norm=qk_norm, v_width=v_width)
    return pl.pallas_call(
        kern,
        grid=(b, s // tm, 3),
        in_specs=[
            pl.BlockSpec((1, tm, d), lambda bi, i, j: (bi, i, 0)),
            pl.BlockSpec((1, d), lambda bi, i, j: (0, 0)),
            pl.BlockSpec((1, 1, d), lambda bi, i, j: (bi, 0, 0)),
            pl.BlockSpec((1, 1, d), lambda bi, i, j: (bi, 0, 0)),
            pl.BlockSpec((d, d), lambda bi, i, j: (0, j)),
            pl.BlockSpec((1, 1, d), lambda bi, i, j: (j, 0, 0)),
        ],
        out_specs=[
            pl.BlockSpec((1, n_ch, tm, HEAD_DIM), lambda bi, i, j: (bi, jnp.minimum(j, 1), i, 0)),
            pl.BlockSpec((1, d // v_width, tm, v_width), lambda bi, i, j: (bi, 0, i, 0)),
        ],
        out_shape=[
            jax.ShapeDtypeStruct((b, 2 * n_ch, s, HEAD_DIM), BF16),
            jax.ShapeDtypeStruct((b, d // v_width, s, v_width), BF16),
        ],
        scratch_shapes=[pltpu.VMEM((tm, d), BF16)],
        compiler_params=_cparams(("arbitrary", "arbitrary", "arbitrary")),
        name="norm_qkv_proj",
    )(x, g.reshape(1, d), sc, sh, w_bf16, colscale)


def _sb_kernel(q_ref, k_ref, v_ref, u_ref, o_ref, *, tq, tk):
    i = pl.program_id(2)
    q = q_ref[0, 0]
    per = tq // tk
    rows = i * tq + lax.broadcasted_iota(I32, (tq, 1), 0)
    u = u_ref[...]


    def weights(c, r0, run, masked):
        start = pl.multiple_of(c * tk, tk)
        z = _dot_nt(q[r0:], k_ref[0, 0, pl.ds(start, tk), :])
        softplus = jnp.maximum(z, 0.0) + jnp.log(1.0 + jnp.exp2(-jnp.abs(z))) * LOG2E
        if masked:
            earlier = (start + lax.broadcasted_iota(I32, (1, tk), 1)) < rows[r0:]
            softplus = jnp.where(earlier, softplus, 0.0)
        between = _dot(softplus.astype(BF16), u) + run[r0:]
        a = jnp.exp2((z - softplus) + between)
        if masked:
            a = jnp.where(earlier, a, 0.0)
        dec = jnp.sum(softplus, axis=-1, keepdims=True)
        run = jnp.concatenate([run[:r0], run[r0:] - dec], axis=0) if r0 else run - dec
        return a.astype(BF16), run

    def values(c, r0, a, acc):
        pv = _dot(a, v_ref[0, 0, pl.ds(pl.multiple_of(c * tk, tk), tk), :])
        return jnp.concatenate([acc[:r0], acc[r0:] + pv], axis=0) if r0 else acc + pv

    def group(chunks, run, acc, masked):
        ws = []
        for c, r0 in chunks:
            a, run = weights(c, r0, run, masked)
            ws.append(a)
        for (c, r0), a in zip(chunks, ws):
            acc = values(c, r0, a, acc)
        return run, acc

    run, acc = group([(per * i + m, m * tk) for m in reversed(range(per))],
                     jnp.zeros((tq, 1), F32), jnp.zeros((tq, HEAD_DIM), F32), True)
    def sweep(top, n, carry):
        return group([(top - r, 0) for r in range(n)], carry[0], carry[1], False)

    odd = i % 2
    top = per * i - 1
    carry = lax.fori_loop(0, odd, lambda _, cr: sweep(top, per, cr), (run, acc))
    top = top - per * odd
    _, acc = lax.fori_loop(0, i // 2, lambda b, cr: sweep(top - b * (2 * per), 2 * per, cr), carry)
    o_ref[0] = acc.astype(BF16)


def _sb_attention(qk, v, d):
    b, _, s, _ = qk.shape
    tq, tk = SB_TQ, SB_TK
    heads = SB_HEADS
    jj = lax.broadcasted_iota(I32, (tk, tk), 0)
    ss = lax.broadcasted_iota(I32, (tk, tk), 1)
    u = jnp.where(jj > ss, -1.0, 0.0).astype(BF16)
    kern = functools.partial(_sb_kernel, tq=tq, tk=tk)
    return pl.pallas_call(
        kern,
        grid=(b, heads, s // tq),
        in_specs=[
            pl.BlockSpec((1, 1, tq, HEAD_DIM), lambda bi, h, i: (bi, h, i, 0)),
            pl.BlockSpec((1, 1, s, HEAD_DIM), lambda bi, h, i: (bi, heads + h, 0, 0)),
            pl.BlockSpec((1, 1, s, HEAD_DIM), lambda bi, h, i: (bi, h, 0, 0)),
            pl.BlockSpec((tk, tk), lambda bi, h, i: (0, 0)),
        ],
        out_specs=pl.BlockSpec((1, tq, HEAD_DIM), lambda bi, h, i: (bi, i, h)),
        out_shape=jax.ShapeDtypeStruct((b, s, d), BF16),
        compiler_params=_cparams(("arbitrary", "arbitrary", "arbitrary")),
        name="stickbreak_attn",
    )(qk, qk, v, u)


def _da_kernel(q_ref, k_ref, v_ref, slope_ref, lam_ref, subln_ref, o_ref, *, tq, tk, lambda_init):
    i = pl.program_id(2)
    row0 = i * tq
    rows = row0 + lax.broadcasted_iota(I32, (tq, 1), 0)
    slope = slope_ref[0]
    qs = (q_ref[0, 0], q_ref[0, 1])
    dv = v_ref.shape[-1]

    def chunk(start, width, carry, masked):
        start = pl.multiple_of(start, tq)
        cols = start + lax.broadcasted_iota(I32, (1, width), 1)
        key_bias = slope * (cols - row0).astype(F32)
        v = v_ref[0, 0, pl.ds(start, width), :]
        new = []
        for m in range(2):
            mx, den, acc = carry[3 * m:3 * m + 3]
            k = k_ref[0, m, pl.ds(start, width), :]
            z = _dot_nt(qs[m], k) + key_bias
            if masked:
                z = jnp.where(cols <= rows, z, -jnp.inf)
            mx_new = jnp.maximum(mx, jnp.max(z, axis=-1, keepdims=True))
            alpha = jnp.exp2(mx - mx_new)
            p = jnp.exp2(z - mx_new)
            den = alpha * den + jnp.sum(p, axis=-1, keepdims=True)
            acc = alpha * acc + _dot(p.astype(BF16), v)
            new += [mx_new, den, acc]
        return tuple(new)

    init = []
    for _ in range(2):
        init += [jnp.full((tq, 1), -jnp.inf, F32), jnp.zeros((tq, 1), F32), jnp.zeros((tq, dv), F32)]
    n_below = row0 // tk
    carry = lax.fori_loop(0, n_below, lambda c, cr: chunk(c * tk, tk, cr, False), tuple(init))
    carry = chunk(n_below * tk, tk, carry, True)
    _, l1, a1, _, l2, a2 = carry

    lv = lam_ref[...]
    s1 = jnp.sum(lv[0:1] * lv[1:2], axis=-1, keepdims=True)
    s2 = jnp.sum(lv[2:3] * lv[3:4], axis=-1, keepdims=True)
    lam = jnp.exp(s1) - jnp.exp(s2) + lambda_init
    o = a1 / l1 - lam * (a2 / l2)
    o = o * lax.rsqrt(jnp.mean(o * o, axis=-1, keepdims=True) + SUBLN_EPS) * subln_ref[...]
    o_ref[0] = (o * (1.0 - lambda_init)).astype(BF16)


def _da_attention(qk, v, lam_rows, subln, d, lambda_init):
    b, _, s, _ = qk.shape
    tq, tk = DA_TQ, DA_TK
    heads = DA_HEADS
    dv = v.shape[-1]
    slopes = jnp.asarray(
        [2.0 ** (-8.0 * (h + 1) / heads) * LOG2E for h in range(heads)], F32).reshape(heads, 1, 1)
    kern = functools.partial(_da_kernel, tq=tq, tk=tk, lambda_init=lambda_init)
    return pl.pallas_call(
        kern,
        grid=(b, heads, s // tq),
        in_specs=[
            pl.BlockSpec((1, 2, tq, HEAD_DIM), lambda bi, h, i: (bi, h, i, 0)),
            pl.BlockSpec((1, 2, s, HEAD_DIM), lambda bi, h, i: (bi, heads + h, 0, 0)),
            pl.BlockSpec((1, 1, s, dv), lambda bi, h, i: (bi, h, 0, 0)),
            pl.BlockSpec((1, 1, 1), lambda bi, h, i: (h, 0, 0)),
            pl.BlockSpec((4, HEAD_DIM), lambda bi, h, i: (0, 0)),
            pl.BlockSpec((1, dv), lambda bi, h, i: (0, 0)),
        ],
        out_specs=pl.BlockSpec((1, tq, dv), lambda bi, h, i: (bi, i, h)),
        out_shape=jax.ShapeDtypeStruct((b, s, d), BF16),
        compiler_params=_cparams(("arbitrary", "arbitrary", "arbitrary")),
        name="diff_attn",
    )(qk, qk, v, slopes, lam_rows, subln.reshape(1, dv))


def _outproj_kernel(o_ref, w_ref, x_ref, g_ref, y_ref):
    y_ref[0] = x_ref[0] + g_ref[0] * _dot(o_ref[0], w_ref[...])


def _out_proj(o, w_bf16, x, gate):
    b, s, d = x.shape
    tm = PROJ_TM
    return pl.pallas_call(
        _outproj_kernel,
        grid=(b, s // tm),
        in_specs=[
            pl.BlockSpec((1, tm, d), lambda bi, i: (bi, i, 0)),
            pl.BlockSpec((d, d), lambda bi, i: (0, 0)),
            pl.BlockSpec((1, tm, d), lambda bi, i: (bi, i, 0)),
            pl.BlockSpec((1, 1, d), lambda bi, i: (bi, 0, 0)),
        ],
        out_specs=pl.BlockSpec((1, tm, d), lambda bi, i: (bi, i, 0)),
        out_shape=jax.ShapeDtypeStruct((b, s, d), F32),
        compiler_params=_cparams(("arbitrary", "arbitrary")),
        name="out_proj_residual",
    )(o, w_bf16, x, gate)


def _router_kernel(x_ref, g_ref, sc_ref, sh_ref, wr_ref, bias_ref, u_ref,
                   h_ref, slab_ref, te_ref, gate_ref, rank_ref, cnt_ref, run_ref, *, tm):
    i = pl.program_id(0)
    per_group = N_EXPERTS // N_GROUPS

    @pl.when(i == 0)
    def _():
        run_ref[...] = jnp.zeros_like(run_ref)

    h = _norm_mod(x_ref[...], g_ref[...], sc_ref[0], sh_ref[0])
    h_ref[...] = h.astype(BF16)
    _store_slabs(slab_ref, 0, tm, h)
    logits = lax.dot_general(wr_ref[0], h, (((1,), (1,)), ((), ())),
                             precision=lax.Precision.HIGHEST, preferred_element_type=F32)
    scores = jax.nn.sigmoid(logits)
    biased = scores + bias_ref[0]
    neg = -jnp.inf
    sub = lax.broadcasted_iota(I32, (per_group, tm), 0)

    def first_max(vals, idx, sentinel):
        m = jnp.max(vals, axis=0, keepdims=True)
        return m, jnp.min(jnp.where(vals == m, idx, sentinel), axis=0, keepdims=True)

    sc_g = [scores[g * per_group:(g + 1) * per_group] for g in range(N_GROUPS)]
    bi_g = [biased[g * per_group:(g + 1) * per_group] for g in range(N_GROUPS)]

    gscore = jnp.zeros((N_GROUPS, tm), F32)
    for g in range(N_GROUPS):
        m1, i1 = first_max(bi_g[g], sub, per_group)
        m2 = jnp.max(jnp.where(sub == i1, neg, bi_g[g]), axis=0, keepdims=True)
        gscore = jnp.where(sub == g, m1 + m2, gscore)

    keep = jnp.zeros((N_GROUPS, tm), F32)
    cur = gscore
    for _ in range(TOPK_GROUPS):
        _, gi = first_max(cur, sub, N_GROUPS)
        hit = sub == gi
        keep = jnp.where(hit, 1.0, keep)
        cur = jnp.where(hit, neg, cur)

    masked = []
    for g in range(N_GROUPS):
        kg = jnp.max(jnp.where(sub == g, keep, 0.0), axis=0, keepdims=True)
        masked.append(jnp.where(kg > 0.0, bi_g[g], neg))
    eidx = [sub + g * per_group for g in range(N_GROUPS)]
    sel = [jnp.zeros((per_group, tm), F32) for _ in range(N_GROUPS)]

    top_e, gates = [], []
    for _ in range(TOP_K):
        m = masked[0].max(axis=0, keepdims=True)
        for g in range(1, N_GROUPS):
            m = jnp.maximum(m, jnp.max(masked[g], axis=0, keepdims=True))
        ei = jnp.full((1, tm), N_EXPERTS, I32)
        for g in range(N_GROUPS):
            ei = jnp.minimum(ei, jnp.min(jnp.where(masked[g] == m, eidx[g], N_EXPERTS), axis=0, keepdims=True))
        gt = jnp.zeros((1, tm), F32)
        for g in range(N_GROUPS):
            hit = eidx[g] == ei
            gt = gt + jnp.sum(jnp.where(hit, sc_g[g], 0.0), axis=0, keepdims=True)
            masked[g] = jnp.where(hit, neg, masked[g])
            sel[g] = jnp.where(hit, 1.0, sel[g])
        top_e.append(ei)
        gates.append(gt)

    gsum = gates[0]
    for gt in gates[1:]:
        gsum = gsum + gt

    sel_all = jnp.concatenate(sel, axis=0)
    before = _dot(sel_all.astype(BF16), u_ref[...]) + run_ref[...]
    run_new = run_ref[...] + jnp.sum(sel_all, axis=1, keepdims=True)
    run_ref[...] = run_new
    cnt_ref[...] = jnp.broadcast_to(run_new, cnt_ref.shape).astype(I32)

    te_ref[...] = jnp.zeros_like(te_ref)
    gate_ref[...] = jnp.zeros_like(gate_ref)
    rank_ref[...] = jnp.zeros_like(rank_ref)
    for j in range(TOP_K):
        rk = jnp.zeros((1, tm), F32)
        for g in range(N_GROUPS):
            rk = rk + jnp.sum(jnp.where(eidx[g] == top_e[j], before[g * per_group:(g + 1) * per_group], 0.0),
                              axis=0, keepdims=True)
        te_ref[j:j + 1, :] = top_e[j]
        gate_ref[j:j + 1, :] = gates[j] / gsum * ROUTED_SCALE
        rank_ref[j:j + 1, :] = rk.astype(I32)


def _route(x2, g, sc, sh, w_router, bias, layer, seq):
    t, d = x2.shape
    tm = ROUTE_TM
    e = N_EXPERTS
    slab = d // LANES
    jj = lax.broadcasted_iota(I32, (tm, tm), 0)
    ss = lax.broadcasted_iota(I32, (tm, tm), 1)
    u = (jj < ss).astype(BF16)
    per_seq = seq // tm
    kern = functools.partial(_router_kernel, tm=tm)
    rows = SUBLANES
    depth = w_router.shape[0]
    h, h_slabs, te, gate, rank, cnt = pl.pallas_call(
        kern,
        grid=(t // tm,),
        in_specs=[
            pl.BlockSpec((tm, d), lambda i: (i, 0)),
            pl.BlockSpec((1, d), lambda i: (0, 0)),
            pl.BlockSpec((1, 1, d), lambda i: (i // per_seq, 0, 0)),
            pl.BlockSpec((1, 1, d), lambda i: (i // per_seq, 0, 0)),
            pl.BlockSpec((1, e, d), lambda i: (layer, 0, 0)),
            pl.BlockSpec((1, e, 1), lambda i: (layer, 0, 0)),
            pl.BlockSpec((tm, tm), lambda i: (0, 0)),
        ],
        out_specs=[
            pl.BlockSpec((tm, d), lambda i: (i, 0)),
            pl.BlockSpec((tm * slab, LANES), lambda i: (i, 0)),
            pl.BlockSpec((rows, tm), lambda i: (0, i)),
            pl.BlockSpec((rows, tm), lambda i: (0, i)),
            pl.BlockSpec((rows, tm), lambda i: (0, i)),
            pl.BlockSpec((e, LANES), lambda i: (0, 0)),
        ],
        out_shape=[
            jax.ShapeDtypeStruct((t, d), BF16),
            jax.ShapeDtypeStruct((t * slab, LANES), F32),
            jax.ShapeDtypeStruct((rows, t), I32),
            jax.ShapeDtypeStruct((rows, t), F32),
            jax.ShapeDtypeStruct((rows, t), I32),
            jax.ShapeDtypeStruct((e, LANES), I32),
        ],
        scratch_shapes=[pltpu.VMEM((e, 1), F32)],
        compiler_params=_cparams(("arbitrary",)),
        name="norm_router_topk",
    )(x2, g.reshape(1, d), sc, sh, w_router, bias.reshape(depth, e, 1), u)
    return h, h_slabs, te[:TOP_K], gate[:TOP_K], rank[:TOP_K], cnt[:, 0]


def _dispatch_kernel(slot_ref, pe_ref, nu_ref, h_ref, hb_ref, wg_ref, wu_ref, wd_ref, xs_hbm, sh_ref,
                     zbuf, sem, zsem, *, tm, blk, n_blocks, slab):
    i = pl.program_id(0)
    n_used = nu_ref[0]
    group = 8

    def zero_copy(block):
        start = pl.multiple_of(block * (blk * slab), blk * slab)
        return pltpu.make_async_copy(zbuf, xs_hbm.at[pl.ds(start, blk * slab)], zsem)

    def expert_has_rows(e):
        return pe_ref[e] > (pe_ref[e - 1] if e else 0)

    @pl.when(i == 0)
    def _():
        zbuf[...] = jnp.zeros_like(zbuf)
        for phase in ("start", "wait"):
            for e in range(N_EXPERTS):
                @pl.when(expert_has_rows(e))
                def _():
                    cp = zero_copy(pe_ref[e] // blk - 1)
                    cp.start() if phase == "start" else cp.wait()

            def tail(bk, carry):
                cp = zero_copy(bk)
                cp.start() if phase == "start" else cp.wait()
                return carry

            lax.fori_loop(n_used, n_blocks, tail, 0)

    base = i * (TOP_K * tm)

    def issue(j):
        def body(g, carry):
            for r in range(group):
                dst = pl.multiple_of(slot_ref[base + j * tm + g * group + r] * slab, slab)
                src = pl.multiple_of(g * (group * slab), group * slab) + r * slab
                pltpu.make_async_copy(h_ref.at[pl.ds(src, slab)], xs_hbm.at[pl.ds(dst, slab)], sem).start(
                    priority=r % 2)
            return carry

        lax.fori_loop(0, tm // group, body, 0)

    for j in range(TOP_K // 2):
        issue(j)
    hb = hb_ref[...]
    mid = _silu(_dot(hb, wg_ref[0])) * _dot(hb, wu_ref[0])
    sh_ref[...] = _dot(mid.astype(BF16), wd_ref[0]).astype(BF16)
    for j in range(TOP_K // 2, TOP_K):
        issue(j)

    for j in range(TOP_K):
        pltpu.make_async_copy(h_ref, xs_hbm.at[pl.ds(0, tm * slab)], sem).wait()


def _dispatch(h_slabs, h, slots_tiled, pad_end, n_used, n_blocks, slab, ws_gate, ws_up, ws_down, layer):
    t, d = h.shape
    f = ws_gate.shape[-1]
    tm = DISP_TM
    blk = MOE_BLK
    kern = functools.partial(_dispatch_kernel, tm=tm, blk=blk, n_blocks=n_blocks, slab=slab)
    grid_spec = pltpu.PrefetchScalarGridSpec(
        num_scalar_prefetch=3,
        grid=(t // tm,),
        in_specs=[
            pl.BlockSpec((tm * slab, LANES), lambda i, sl, pe, nu: (i, 0)),
            pl.BlockSpec((tm, d), lambda i, sl, pe, nu: (i, 0)),
            pl.BlockSpec((1, d, f), lambda i, sl, pe, nu: (layer, 0, 0)),
            pl.BlockSpec((1, d, f), lambda i, sl, pe, nu: (layer, 0, 0)),
            pl.BlockSpec((1, f, d), lambda i, sl, pe, nu: (layer, 0, 0)),
        ],
        out_specs=[
            pl.BlockSpec(memory_space=pl.ANY),
            pl.BlockSpec((tm, d), lambda i, sl, pe, nu: (i, 0)),
        ],
        scratch_shapes=[
            pltpu.VMEM((blk * slab, LANES), F32),
            pltpu.SemaphoreType.DMA,
            pltpu.SemaphoreType.DMA,
        ],
    )
    return pl.pallas_call(
        kern,
        grid_spec=grid_spec,
        out_shape=[
            jax.ShapeDtypeStruct((n_blocks * blk * slab, LANES), F32),
            jax.ShapeDtypeStruct((t, d), BF16),
        ],
        compiler_params=_cparams(("arbitrary",)),
        name="dispatch_rows",
    )(slots_tiled, pad_end, n_used, h_slabs, h, ws_gate, ws_up, ws_down)


def _expert_kernel(be_ref, ne_ref, nu_ref, x_ref, wg_hbm, wu_hbm, wd_hbm, y_ref,
                   wg32, wu32, wd32, wgb, wub, wdb, sem, *, blk, slab, layer):
    i = pl.program_id(0)
    n_used = nu_ref[0]

    def weight_copies(e):
        return [pltpu.make_async_copy(src.at[layer, e], dst, sem)
                for src, dst in ((wg_hbm, wg32), (wu_hbm, wu32), (wd_hbm, wd32))]

    @pl.when(i == 0)
    def _():
        for cp in weight_copies(be_ref[0]):
            cp.start()

    @pl.when(i < n_used)
    def _():
        prev = be_ref[jnp.maximum(i - 1, 0)]

        @pl.when((i == 0) | (be_ref[i] != prev))
        def _():
            for cp in weight_copies(be_ref[i]):
                cp.wait()
            wgb[...] = wg32[...].astype(BF16)
            wub[...] = wu32[...].astype(BF16)
            wdb[...] = wd32[...].astype(BF16)
            nxt = ne_ref[i]

            @pl.when(nxt >= 0)
            def _():
                for cp in weight_copies(nxt):
                    cp.start()

        x = jnp.concatenate([c.astype(BF16) for c in _load_slabs(x_ref, 0, blk, slab)], axis=1)
        mid = _silu(_dot(x, wgb[...])) * _dot(x, wub[...])
        _store_slabs(y_ref, 0, blk, _dot(mid.astype(BF16), wdb[...]))

    @pl.when(i >= n_used)
    def _():
        y_ref[...] = jnp.zeros_like(y_ref)


def _experts(xs, w_gate, w_up, w_down, layer, block_e, next_expert, n_used, slab):
    d, f = w_gate.shape[-2:]
    blk = MOE_BLK
    n_blocks = xs.shape[0] // (blk * slab)
    kern = functools.partial(_expert_kernel, blk=blk, slab=slab, layer=layer)
    grid_spec = pltpu.PrefetchScalarGridSpec(
        num_scalar_prefetch=3,
        grid=(n_blocks,),
        in_specs=[
            pl.BlockSpec((blk * slab, LANES), lambda i, be, ne, nu: (jnp.minimum(i, nu[0] - 1), 0)),
            pl.BlockSpec(memory_space=pl.ANY),
            pl.BlockSpec(memory_space=pl.ANY),
            pl.BlockSpec(memory_space=pl.ANY),
        ],
        out_specs=pl.BlockSpec((blk * slab, LANES), lambda i, be, ne, nu: (i, 0)),
        scratch_shapes=[
            pltpu.VMEM((d, f), F32),
            pltpu.VMEM((d, f), F32),
            pltpu.VMEM((f, d), F32),
            pltpu.VMEM((d, f), BF16),
            pltpu.VMEM((d, f), BF16),
            pltpu.VMEM((f, d), BF16),
            pltpu.SemaphoreType.DMA,
        ],
    )
    return pl.pallas_call(
        kern,
        grid_spec=grid_spec,
        out_shape=jax.ShapeDtypeStruct(xs.shape, F32),
        compiler_params=_cparams(("arbitrary",)),
        name="routed_experts",
    )(block_e, next_expert, n_used, xs, w_gate, w_up, w_down)


def _combine_kernel(slot_ref, y_hbm, sh_ref, x_ref, g_ref, gw_ref, o_ref, ybuf, sem, *, tm, slab, pitch):
    i = pl.program_id(0)
    n = pl.num_programs(0)
    rows = TOP_K * tm
    group = 8

    def issue(tile, slot):
        base = tile * rows

        def body(g, carry):
            for r in range(group):
                src = pl.multiple_of(slot_ref[base + g * group + r] * slab, slab)
                dst = pl.multiple_of((slot * rows + g * group) * pitch, group * pitch) + r * pitch
                pltpu.make_async_copy(y_hbm.at[pl.ds(src, slab)], ybuf.at[pl.ds(dst, slab)], sem.at[slot]).start(
                    priority=r % 2)
            return carry

        lax.fori_loop(0, rows // group, body, 0)

    @pl.when(i == 0)
    def _():
        issue(0, 0)

    @pl.when(i + 1 < n)
    def _():
        issue(i + 1, (i + 1) % 2)

    slot = i % 2
    buf0 = pl.multiple_of(slot * (rows * pitch), rows * pitch)
    pltpu.make_async_copy(y_hbm.at[pl.ds(0, rows * slab)], ybuf.at[pl.ds(0, rows * slab)], sem.at[slot]).wait()
    gw = gw_ref[...]
    gwb = [jnp.broadcast_to(gw[:, j:j + 1], (tm, LANES)) for j in range(TOP_K)]
    chunks = [jnp.zeros((tm, LANES), F32) for _ in range(slab)]
    for j in range(TOP_K):
        for s, y in enumerate(_load_slabs(ybuf, buf0 + j * (tm * pitch), tm, slab, pitch)):
            chunks[s] = chunks[s] + gwb[j] * y
    routed = jnp.concatenate(chunks, axis=1)
    o_ref[...] = x_ref[...] + g_ref[0] * (sh_ref[...].astype(F32) + routed)


def _combine(slots_tiled, y_slots, shared, x2, gate, gate_w, seq, slab):
    t, d = x2.shape
    tm = COMB_TM
    per_seq = seq // tm
    pitch = slab + SUBLANES
    kern = functools.partial(_combine_kernel, tm=tm, slab=slab, pitch=pitch)
    grid_spec = pltpu.PrefetchScalarGridSpec(
        num_scalar_prefetch=1,
        grid=(t // tm,),
        in_specs=[
            pl.BlockSpec(memory_space=pl.ANY),
            pl.BlockSpec((tm, d), lambda i, sl: (i, 0)),
            pl.BlockSpec((tm, d), lambda i, sl: (i, 0)),
            pl.BlockSpec((1, 1, d), lambda i, sl: (i // per_seq, 0, 0)),
            pl.BlockSpec((tm, SUBLANES), lambda i, sl: (i, 0)),
        ],
        out_specs=pl.BlockSpec((tm, d), lambda i, sl: (i, 0)),
        scratch_shapes=[
            pltpu.VMEM((2 * TOP_K * tm * pitch, LANES), F32),
            pltpu.SemaphoreType.DMA((2,)),
        ],
    )
    return pl.pallas_call(
        kern,
        grid_spec=grid_spec,
        out_shape=jax.ShapeDtypeStruct((t, d), F32),
        compiler_params=_cparams(("arbitrary",)),
        name="gather_combine",
    )(slots_tiled, y_slots, shared, x2, gate, gate_w)


def _tile_slots(slot, tm):
    k, t = slot.shape
    return slot.reshape(k, t // tm, tm).transpose(1, 0, 2).reshape(-1)


def _moe_layer(x, g, sc, sh, gate, w_router, router_bias, w_gate, w_up, w_down,
               ws_gate_b, ws_up_b, ws_down_b, layer):
    b, s, d = x.shape
    t = b * s
    x2 = x.reshape(t, d)
    slab = d // LANES
    h, h_slabs, top_e, gates, rank, counts = _route(x2, g, sc, sh, w_router, router_bias, layer, s)

    blk = MOE_BLK
    n_blocks = -(-(t * TOP_K) // blk) + N_EXPERTS
    padded = (counts + blk - 1) // blk * blk
    pad_end = jnp.cumsum(padded).astype(I32)
    pad_start = pad_end - padded
    experts = jnp.arange(N_EXPERTS, dtype=I32)
    slot = jnp.sum(jnp.where(top_e[..., None] == experts, pad_start, 0), axis=-1) + rank
    block_first = jnp.arange(n_blocks, dtype=I32) * blk
    block_e = jnp.minimum(jnp.sum((pad_end[None, :] <= block_first[:, None]).astype(I32), axis=1), N_EXPERTS - 1)
    n_used = pad_end[-1:] // blk
    has_rows = padded > 0
    later = jnp.where(has_rows[None, :] & (experts[None, :] > experts[:, None]), experts[None, :], N_EXPERTS)
    following = jnp.min(later, axis=1)
    following = jnp.where(following < N_EXPERTS, following, -1)
    block_hot = block_e[:, None] == experts[None, :]
    next_expert = jnp.sum(jnp.where(block_hot, following[None, :], 0), axis=1).astype(I32)

    xs, shared = _dispatch(h_slabs, h, _tile_slots(slot, DISP_TM), pad_end, n_used, n_blocks, slab,
                           ws_gate_b, ws_up_b, ws_down_b, layer)
    y_slots = _experts(xs, w_gate, w_up, w_down, layer, block_e, next_expert, n_used, slab)
    gate_w = jnp.pad(gates.T, ((0, 0), (0, SUBLANES - TOP_K)))
    out = _combine(_tile_slots(slot, COMB_TM), y_slots, shared, x2, gate, gate_w, s, slab)
    return out.reshape(b, s, d)


def kernel(x, c, w_mod, b_mod, norm_mix, norm_ffn, w_in, w_out, q_norm, k_norm, lambda_q1, lambda_k1,
           lambda_q2, lambda_k2, subln, w_router, router_bias, w_gate, w_up, w_down, ws_gate, ws_up, ws_down):
    b, s, d = x.shape
    depth = w_mod.shape[0]
    assert d == SB_HEADS * HEAD_DIM == 2 * DA_HEADS * HEAD_DIM
    assert s % max(SB_TQ, SB_TK, DA_TQ, DA_TK, PROJ_TM, ROUTE_TM, DISP_TM) == 0
    assert SB_TQ % SB_TK == 0 and DA_TK % DA_TQ == 0
    q_scale = HEAD_DIM ** -0.5 * LOG2E

    mod = _modulation(c, w_mod, b_mod)
    ws_gate_b, ws_up_b, ws_down_b = ws_gate.astype(BF16), ws_up.astype(BF16), ws_down.astype(BF16)
    for i in range(depth):
        sh1, sc1, g1, sh2, sc2, g2 = [m.reshape(b, 1, d) for m in jnp.split(mod[i], 6, axis=-1)]
        w_in_b = w_in[i].astype(BF16)
        if i % 2 == 0:
            colscale = jnp.stack([jnp.full((1, d), q_scale, F32), jnp.ones((1, d), F32), jnp.ones((1, d), F32)])
            qk, v = _qkv_proj(x, norm_mix[i], sc1, sh1, w_in_b, colscale, qk_norm=False, v_width=HEAD_DIM)
            o = _sb_attention(qk, v, d)
        else:
            j = i // 2
            lambda_init = 0.8 - 0.6 * math.exp(-0.3 * i)
            reps = d // HEAD_DIM
            colscale = jnp.stack([
                jnp.tile(q_norm[j].astype(F32), reps)[None, :] * q_scale,
                jnp.tile(k_norm[j].astype(F32), reps)[None, :],
                jnp.ones((1, d), F32)])
            qk, v = _qkv_proj(x, norm_mix[i], sc1, sh1, w_in_b, colscale, qk_norm=True, v_width=2 * HEAD_DIM)
            lam_rows = jnp.stack([lambda_q1[j], lambda_k1[j], lambda_q2[j], lambda_k2[j]]).astype(F32)
            o = _da_attention(qk, v, lam_rows, subln[j].astype(F32), d, lambda_init)
        x = _out_proj(o, w_out[i].astype(BF16), x, g1)
        x = _moe_layer(x, norm_ffn[i], sc2, sh2, g2, w_router, router_bias, w_gate, w_up, w_down,
                       ws_gate_b, ws_up_b, ws_down_b, i)
    return x
```
